```python
import jax, jax.numpy as jnp
from jax import lax
import numpy as np

D_MODEL = 1024
BATCH = 32
SEQ = 256
DEPTH = 1
DEC_BATCH = 8
DEC_SEQ = 4096
PAST_LEN = 512

GRID_W = 64
MIX_WIDTH = 2 * D_MODEL
SSD_WIDTH = D_MODEL
SSD_HEAD_DIM = 64
SSD_HEADS = SSD_WIDTH // SSD_HEAD_DIM
N_GROUPS = 2
D_STATE = 128
SSD_CONV = 3
CHUNK = 128
CONV_CH = SSD_WIDTH + 2 * N_GROUPS * D_STATE
CF_WIDTH = MIX_WIDTH - SSD_WIDTH
CF_KERNEL = 31
IN_COLS = SSD_WIDTH + CONV_CH + 2 * SSD_HEADS + 2 * CF_WIDTH
D_FF = 2816
FFN_CONV = 3
EPS = 1e-6

kernel_name = "hybrid_ssd_conformer_diffusion_step"


def rms_norm(x, g):
    xf = x.astype(jnp.float32)
    y = xf * lax.rsqrt(jnp.mean(xf * xf, axis=-1, keepdims=True) + EPS)
    return (y * g.astype(jnp.float32)).astype(x.dtype)


def layer_norm(x, g, b):
    xf = x.astype(jnp.float32)
    mu = jnp.mean(xf, axis=-1, keepdims=True)
    var = jnp.mean(jnp.square(xf - mu), axis=-1, keepdims=True)
    y = (xf - mu) * lax.rsqrt(var + EPS)
    return (y * g.astype(jnp.float32) + b.astype(jnp.float32)).astype(x.dtype)


def dwconv1d(x, w, bias):
    k = w.shape[0]
    out = lax.conv_general_dilated(
        x, w.astype(x.dtype)[:, None, :], window_strides=(1,),
        padding=[((k - 1) // 2, k // 2)],
        dimension_numbers=("NWC", "WIO", "NWC"),
        feature_group_count=x.shape[-1])
    return out + bias.astype(x.dtype)


def dwconv2d(x, w, bias, rows):
    bsz, seqlen, ch = x.shape
    xi = x.reshape(bsz, rows, GRID_W, ch)
    out = lax.conv_general_dilated(
        xi, w.astype(x.dtype)[:, :, None, :], window_strides=(1, 1), padding="SAME",
        dimension_numbers=("NHWC", "HWIO", "NHWC"),
        feature_group_count=ch)
    return out.reshape(bsz, seqlen, ch) + bias.astype(x.dtype)


def ssd_chunked(x, dt, a, b_mat, c_mat, init_state):
    bsz, seqlen, h, p = x.shape
    g, n = b_mat.shape[2], b_mat.shape[3]
    r = h // g
    nc = seqlen // CHUNK
    xc = x.reshape(bsz, nc, CHUNK, g, r, p)
    dtc = dt.reshape(bsz, nc, CHUNK, g, r)
    bc = b_mat.reshape(bsz, nc, CHUNK, g, n)
    cc = c_mat.reshape(bsz, nc, CHUNK, g, n)
    a_cum = jnp.cumsum(dtc * a.reshape(g, r), axis=2)
    xdt = xc * dtc[..., None]
    idx = jnp.arange(CHUNK)
    lower = (idx[:, None] >= idx[None, :])[:, :, None, None]
    seg = a_cum[:, :, :, None] - a_cum[:, :, None, :]
    decay = jnp.exp(jnp.where(lower, seg, -jnp.inf))
    cb = jnp.einsum("bcign,bcjgn->bcijg", cc, bc)
    y_diag = jnp.einsum("bcijg,bcijgr,bcjgrp->bcigrp", cb, decay, xdt)
    decay_to_end = jnp.exp(a_cum[:, :, -1:] - a_cum)
    chunk_states = jnp.einsum("bcjgn,bcjgr,bcjgrp->bcgrpn", bc, decay_to_end, xdt)
    chunk_decay = jnp.exp(a_cum[:, :, -1])

    def step(state, inp):
        cs, cd = inp
        return state * cd[..., None, None] + cs, state

    s0 = init_state.astype(jnp.float32).reshape(bsz, g, r, p, n)
    final, prev = lax.scan(step, s0, (jnp.moveaxis(chunk_states.astype(jnp.float32), 1, 0),
                                      jnp.moveaxis(chunk_decay, 1, 0)))
    prev = jnp.moveaxis(prev, 0, 1)
    y_off = jnp.einsum("bcign,bcgrpn,bcigr->bcigrp", cc, prev, jnp.exp(a_cum))
    y = (y_diag + y_off).reshape(bsz, seqlen, h, p).astype(x.dtype)
    return y, final.reshape(bsz, h, p, n)


def trunk_layer(x, cond, prm, init_f, init_b, rows):
    bsz, seqlen, _ = x.shape
    mod = jax.nn.silu(cond) @ prm["w_ada"] + prm["b_ada"]
    sh1, sc1, g1, sh2, sc2, g2 = [m[:, None, :] for m in jnp.split(mod, 6, axis=-1)]

    h = rms_norm(x, prm["norm_mix_pre"]) * (1 + sc1) + sh1
    proj = h @ prm["w_in"]
    z, xbc, dt_raw, cf_in = jnp.split(
        proj, [SSD_WIDTH, SSD_WIDTH + CONV_CH, SSD_WIDTH + CONV_CH + 2 * SSD_HEADS], axis=-1)

    xbc = jax.nn.silu(dwconv1d(xbc, prm["w_ssd_conv"], prm["b_ssd_conv"]))
    xs, b_mat, c_mat = jnp.split(xbc, [SSD_WIDTH, SSD_WIDTH + N_GROUPS * D_STATE], axis=-1)
    xs = xs.reshape(bsz, seqlen, SSD_HEADS, SSD_HEAD_DIM)
    b_mat = b_mat.reshape(bsz, seqlen, N_GROUPS, D_STATE)
    c_mat = c_mat.reshape(bsz, seqlen, N_GROUPS, D_STATE)
    dt_f, dt_b = jnp.split(dt_raw.astype(jnp.float32), 2, axis=-1)
    dt_f = jax.nn.softplus(dt_f + prm["dt_bias_fwd"].astype(jnp.float32))
    dt_b = jax.nn.softplus(dt_b + prm["dt_bias_bwd"].astype(jnp.float32))
    a_f = -jnp.exp(prm["a_log_fwd"].astype(jnp.float32))
    a_b = -jnp.exp(prm["a_log_bwd"].astype(jnp.float32))
    y_f, s_f = ssd_chunked(xs, dt_f, a_f, b_mat, c_mat, init_f)
    y_b, s_b = ssd_chunked(jnp.flip(xs, 1), jnp.flip(dt_b, 1), a_b,
                           jnp.flip(b_mat, 1), jnp.flip(c_mat, 1), init_b)
    y = y_f + jnp.flip(y_b, 1) + xs * prm["d_skip"][:, None]
    y = y.reshape(bsz, seqlen, SSD_WIDTH)
    y_ssd = rms_norm(y * jax.nn.silu(z), prm["ssd_norm"])

    cf_a, cf_g = jnp.split(cf_in, 2, axis=-1)
    u = cf_a * jax.nn.sigmoid(cf_g)
    u = dwconv1d(u, prm["w_cf_conv"], prm["b_cf_conv"])
    u = jax.nn.silu(layer_norm(u, prm["cf_ln_g"], prm["cf_ln_b"]))

    mix = jnp.concatenate([y_ssd, u], axis=-1) @ prm["w_out"]
    x = x + g1 * rms_norm(mix, prm["norm_mix_post"])

    h = rms_norm(x, prm["norm_ffn_pre"]) * (1 + sc2) + sh2
    up = h @ prm["w_ffn_up"]
    if rows is None:
        up = dwconv1d(up, prm["w_ffn_conv"][1], prm["b_ffn_conv"])
    else:
        up = dwconv2d(up, prm["w_ffn_conv"], prm["b_ffn_conv"], rows)
    f_gate, f_val = jnp.split(up, 2, axis=-1)
    f = (jax.nn.silu(f_gate) * f_val) @ prm["w_ffn_down"]
    x = x + g2 * rms_norm(f, prm["norm_ffn_post"])
    return x, s_f, s_b


def setup_inputs(seed: int = 0) -> dict:
    key = jax.random.key(seed)
    ks = jax.random.split(key, 32)
    f32 = jnp.float32

    def nrm(k, shape, scale):
        return jax.random.normal(k, shape, f32) * scale

    def gain(k, shape):
        return 1.0 + 0.05 * jax.random.normal(k, shape, f32)

    dt0 = jnp.exp(jax.random.uniform(ks[10], (DEPTH, SSD_HEADS), f32, np.log(1e-3), np.log(1e-1)))
    dt1 = jnp.exp(jax.random.uniform(ks[11], (DEPTH, SSD_HEADS), f32, np.log(1e-3), np.log(1e-1)))
    state_shape = (DEC_BATCH, DEPTH, SSD_HEADS, SSD_HEAD_DIM, D_STATE)
    return {
        "x_prompt": nrm(ks[0], (BATCH, SEQ, D_MODEL), 1.0),
        "x_sample": nrm(ks[1], (DEC_BATCH, DEC_SEQ, D_MODEL), 1.0),
        "state_ssd_fwd": nrm(ks[2], state_shape, 0.5),
        "state_ssd_bwd": nrm(ks[3], state_shape, 0.5),
        "c": nrm(ks[4], (DEC_BATCH, D_MODEL), 1.0),
        "c_ctx": nrm(ks[5], (D_MODEL,), 1.0),
        "w_ada": nrm(ks[6], (DEPTH, D_MODEL, 6 * D_MODEL), D_MODEL ** -0.5),
        "b_ada": nrm(ks[7], (DEPTH, 6 * D_MODEL), 0.02),
        "norm_mix_pre": gain(ks[8], (DEPTH, D_MODEL)),
        "norm_mix_post": gain(ks[9], (DEPTH, D_MODEL)),
        "w_in": nrm(ks[12], (DEPTH, D_MODEL, IN_COLS), D_MODEL ** -0.5),
        "w_ssd_conv": nrm(ks[13], (DEPTH, SSD_CONV, CONV_CH), SSD_CONV ** -0.5),
        "b_ssd_conv": nrm(ks[14], (DEPTH, CONV_CH), 0.02),
        "a_log_fwd": jnp.log(jax.random.uniform(ks[15], (DEPTH, SSD_HEADS), f32, 1.0, 16.0)),
        "a_log_bwd": jnp.log(jax.random.uniform(ks[16], (DEPTH, SSD_HEADS), f32, 1.0, 16.0)),
        "dt_bias_fwd": dt0 + jnp.log(-jnp.expm1(-dt0)),
        "dt_bias_bwd": dt1 + jnp.log(-jnp.expm1(-dt1)),
        "d_skip": gain(ks[17], (DEPTH, SSD_HEADS)),
        "ssd_norm": gain(ks[18], (DEPTH, SSD_WIDTH)),
        "w_cf_conv": nrm(ks[19], (DEPTH, CF_KERNEL, CF_WIDTH), CF_KERNEL ** -0.5),
        "b_cf_conv": nrm(ks[20], (DEPTH, CF_WIDTH), 0.02),
        "cf_ln_g": gain(ks[21], (DEPTH, CF_WIDTH)),
        "cf_ln_b": nrm(ks[22], (DEPTH, CF_WIDTH), 0.02),
        "w_out": nrm(ks[23], (DEPTH, MIX_WIDTH, D_MODEL), MIX_WIDTH ** -0.5),
        "norm_ffn_pre": gain(ks[24], (DEPTH, D_MODEL)),
        "norm_ffn_post": gain(ks[25], (DEPTH, D_MODEL)),
        "w_ffn_up": nrm(ks[26], (DEPTH, D_MODEL, 2 * D_FF), D_MODEL ** -0.5),
        "w_ffn_conv": nrm(ks[27], (DEPTH, FFN_CONV, FFN_CONV, 2 * D_FF), 1.0 / FFN_CONV),
        "b_ffn_conv": nrm(ks[28], (DEPTH, 2 * D_FF), 0.02),
        "w_ffn_down": nrm(ks[29], (DEPTH, D_FF, D_MODEL), D_FF ** -0.5),
    }


def reference(x_prompt, x_sample, state_ssd_fwd, state_ssd_bwd, c, c_ctx,
              w_ada, b_ada, norm_mix_pre, norm_mix_post, w_in, w_ssd_conv, b_ssd_conv,
              a_log_fwd, a_log_bwd, dt_bias_fwd, dt_bias_bwd, d_skip, ssd_norm,
              w_cf_conv, b_cf_conv, cf_ln_g, cf_ln_b, w_out, norm_ffn_pre, norm_ffn_post,
              w_ffn_up, w_ffn_conv, b_ffn_conv, w_ffn_down):
    n_ctx_batch = x_prompt.shape[0]
    rows = x_sample.shape[1] // GRID_W
    cond_ctx = jnp.broadcast_to(c_ctx[None, :], (n_ctx_batch, c_ctx.shape[0]))
    zero_state = jnp.zeros((n_ctx_batch, SSD_HEADS, SSD_HEAD_DIM, D_STATE), jnp.float32)

    xp = x_prompt
    xl = x_sample
    new_f, new_b = [], []
    for l in range(DEPTH):
        prm = {
            "w_ada": w_ada[l], "b_ada": b_ada[l],
            "norm_mix_pre": norm_mix_pre[l], "norm_mix_post": norm_mix_post[l],
            "w_in": w_in[l], "w_ssd_conv": w_ssd_conv[l], "b_ssd_conv": b_ssd_conv[l],
            "a_log_fwd": a_log_fwd[l], "a_log_bwd": a_log_bwd[l],
            "dt_bias_fwd": dt_bias_fwd[l], "dt_bias_bwd": dt_bias_bwd[l],
            "d_skip": d_skip[l], "ssd_norm": ssd_norm[l],
            "w_cf_conv": w_cf_conv[l], "b_cf_conv": b_cf_conv[l],
            "cf_ln_g": cf_ln_g[l], "cf_ln_b": cf_ln_b[l], "w_out": w_out[l],
            "norm_ffn_pre": norm_ffn_pre[l], "norm_ffn_post": norm_ffn_post[l],
            "w_ffn_up": w_ffn_up[l], "w_ffn_conv": w_ffn_conv[l],
            "b_ffn_conv": b_ffn_conv[l], "w_ffn_down": w_ffn_down[l],
        }
        xp, s_f, s_b = trunk_layer(xp, cond_ctx, prm, zero_state, zero_state, None)
        new_f.append(s_f)
        new_b.append(s_b)
        xl, _, _ = trunk_layer(xl, c, prm, state_ssd_fwd[:, l], state_ssd_bwd[:, l], rows)

    new_state_ssd_fwd = jnp.stack(new_f, axis=1)
    new_state_ssd_bwd = jnp.stack(new_b, axis=1)
    return (xp, xl, new_state_ssd_fwd, new_state_ssd_bwd)
```

```python
import functools

import jax
import jax.numpy as jnp
from jax import lax
from jax.experimental import pallas as pl
from jax.experimental.pallas import tpu as pltpu

D_MODEL = 1024
GRID_W = 64
SSD_WIDTH = 1024
SSD_HEAD_DIM = 64
SSD_HEADS = 16
N_GROUPS = 2
D_STATE = 128
CHUNK = 128
CONV_CH = SSD_WIDTH + 2 * N_GROUPS * D_STATE
CF_WIDTH = 1024
CF_KERNEL = 31
D_FF = 2816
EPS = 1e-6

LANES = 128
HALO = 16
VMEM_LIMIT = 56 * 1024 * 1024

F32 = jnp.float32
BF16 = jnp.bfloat16
HIGHEST = lax.Precision.HIGHEST


def _params(n_axes):
    return pltpu.CompilerParams(
        dimension_semantics=("arbitrary",) * n_axes, vmem_limit_bytes=VMEM_LIMIT)


def _const_spec(shape):
    nd = len(shape)
    return pl.BlockSpec(shape, lambda *_: (0,) * nd, pipeline_mode=pl.Buffered(1))


def _silu(v):
    return v * jax.nn.sigmoid(v)


def _softplus(v):
    return jnp.maximum(v, 0.0) + jnp.log1p(jnp.exp(-jnp.abs(v)))


def _dot(a, b):
    return jnp.dot(a, b, preferred_element_type=F32)


def _mod_kernel(c_ref, w_ref, b_ref, o_ref):
    s = _silu(c_ref[...])
    o_ref[...] = jnp.dot(s, w_ref[...], precision=HIGHEST,
                         preferred_element_type=F32) + b_ref[...]


def _modulation(cond, w_ada, b_ada):
    rows = cond.shape[0]
    n = w_ada.shape[1]
    tn = 1024
    return pl.pallas_call(
        _mod_kernel,
        grid=(n // tn,),
        in_specs=[_const_spec((rows, D_MODEL)),
                  pl.BlockSpec((D_MODEL, tn), lambda j: (0, j)),
                  pl.BlockSpec((1, tn), lambda j: (0, j))],
        out_specs=pl.BlockSpec((rows, tn), lambda j: (0, j)),
        out_shape=jax.ShapeDtypeStruct((rows, n), F32),
        compiler_params=_params(1),
        name="modulation",
    )(cond, w_ada, b_ada.reshape(1, n))


def _in_proj_kernel(x_ref, sc_ref, sh_ref, g_ref,
                    wz_ref, wx_ref, wdf_ref, wdb_ref, wdft_ref, wdbt_ref, wa_ref, wg_ref,
                    bf_ref, bb_ref, bft_ref, bbt_ref,
                    z_ref, xbc_ref, dtf_ref, dtb_ref, dtft_ref, dtbt_ref, u_ref):
    x = x_ref[...]
    ms = jnp.mean(x * x, axis=-1, keepdims=True)
    h = x * lax.rsqrt(ms + EPS) * (g_ref[...] * (1.0 + sc_ref[...])) + sh_ref[...]
    hb = h.astype(BF16)
    z_ref[...] = _dot(hb, wz_ref[...]).astype(BF16)
    xbc_ref[...] = _dot(hb, wx_ref[...]).astype(BF16)
    dtf_ref[...] = _softplus(_dot(hb, wdf_ref[...]) + bf_ref[...])
    dtb_ref[...] = _softplus(_dot(hb, wdb_ref[...]) + bb_ref[...])
    nt = (((1,), (1,)), ((), ()))
    dtft_ref[...] = _softplus(
        lax.dot_general(wdft_ref[...], hb, nt, preferred_element_type=F32) + bft_ref[...])
    dtbt_ref[...] = _softplus(
        lax.dot_general(wdbt_ref[...], hb, nt, preferred_element_type=F32) + bbt_ref[...])
    a = _dot(hb, wa_ref[...])
    g = _dot(hb, wg_ref[...])
    u_ref[...] = (a * jax.nn.sigmoid(g)).astype(BF16)


def _in_proj(x2d, mod_row, tm, sc, sh, g, w):
    n = x2d.shape[0]
    mod_spec = pl.BlockSpec((None, 1, D_MODEL), lambda i: (mod_row(i), 0, 0))
    tok = lambda width: pl.BlockSpec((tm, width), lambda i: (i, 0))
    tok_t = pl.BlockSpec((SSD_HEADS, tm), lambda i: (0, i))
    weights = [w["wz"], w["wx"], w["wdf"], w["wdb"], w["wdft"], w["wdbt"], w["wa"], w["wg"],
               w["bf"], w["bb"], w["bft"], w["bbt"]]
    return pl.pallas_call(
        _in_proj_kernel,
        grid=(n // tm,),
        in_specs=[tok(D_MODEL), mod_spec, mod_spec, _const_spec((1, D_MODEL))]
                 + [_const_spec(a.shape) for a in weights],
        out_specs=[tok(SSD_WIDTH), tok(CONV_CH), tok(LANES), tok(LANES), tok_t, tok_t,
                   tok(CF_WIDTH)],
        out_shape=[jax.ShapeDtypeStruct((n, SSD_WIDTH), BF16),
                   jax.ShapeDtypeStruct((n, CONV_CH), BF16),
                   jax.ShapeDtypeStruct((n, LANES), F32),
                   jax.ShapeDtypeStruct((n, LANES), F32),
                   jax.ShapeDtypeStruct((SSD_HEADS, n), F32),
                   jax.ShapeDtypeStruct((SSD_HEADS, n), F32),
                   jax.ShapeDtypeStruct((n, CF_WIDTH), BF16)],
        compiler_params=_params(1),
        name="in_proj",
    )(x2d, sc, sh, g, *weights)


def _ssd_kernel(fwd, zero_init, nc, *refs):
    refs = list(refs)
    (xbc_ref, xprev_ref, xnext_ref, dt_ref, dtt_ref, alr_ref, alc_ref, cw_ref,
     cb_ref) = refs[:9]
    del refs[:9]
    init_ref = None if zero_init else refs.pop(0)
    if fwd:
        yb_ref, z_ref, dskip_ref, nrm_ref = refs[:4]
        del refs[:4]
    y_ref, fin_ref, s_ref, xs_ref, yacc_ref = refs
    j = pl.program_id(1)
    c = j if fwd else nc - 1 - j

    @pl.when(j == 0)
    def _():
        if zero_init:
            s_ref[...] = jnp.zeros_like(s_ref)
        else:
            s_ref[...] = init_ref[...]

    x = xbc_ref[...].astype(F32)
    prev_row = jnp.where(c > 0, xprev_ref[...].astype(F32)[HALO - 1:HALO, :], 0.0)
    next_row = jnp.where(c < nc - 1, xnext_ref[...].astype(F32)[0:1, :], 0.0)
    rid = lax.broadcasted_iota(jnp.int32, (CHUNK, 1), 0)
    xm1 = jnp.where(rid == 0, prev_row, pltpu.roll(x, 1, axis=0))
    xp1 = jnp.where(rid == CHUNK - 1, next_row, pltpu.roll(x, CHUNK - 1, axis=0))
    cw = cw_ref[...]
    xc = _silu(cw[0:1, :] * xm1 + cw[1:2, :] * x + cw[2:3, :] * xp1 + cb_ref[...])
    xs_ref[...] = xc[:, :SSD_WIDTH].astype(BF16)
    if fwd:
        yacc_ref[...] = xc[:, :SSD_WIDTH] * dskip_ref[...]
    else:
        yacc_ref[...] = jnp.zeros_like(yacc_ref)
    b_mat = xc[:, SSD_WIDTH:SSD_WIDTH + N_GROUPS * D_STATE]
    c_mat = xc[:, SSD_WIDTH + N_GROUPS * D_STATE:]

    ri = lax.broadcasted_iota(jnp.int32, (CHUNK, CHUNK), 0)
    ci = lax.broadcasted_iota(jnp.int32, (CHUNK, CHUNK), 1)
    keep = (ri >= ci) if fwd else (ri <= ci)
    tri = keep.astype(F32)
    lane = lax.broadcasted_iota(jnp.int32, (1, LANES), 1)
    a_row = jnp.where(lane < SSD_HEADS, -jnp.exp(alr_ref[...]), 0.0)
    a_col = -jnp.exp(alc_ref[...])
    dt_row = dtt_ref[...]
    cum_col = jnp.dot(tri, dt_ref[...] * a_row, precision=HIGHEST,
                      preferred_element_type=F32)
    cum_row = lax.dot_general(dt_row * a_col, tri, (((1,), (1,)), ((), ())),
                              precision=HIGHEST, preferred_element_type=F32)
    end = CHUNK - 1 if fwd else 0
    cum_end = cum_row[:, end:end + 1]
    wgt_row = jnp.exp(cum_end - cum_row) * dt_row
    edec = jnp.exp(cum_end)
    ecol = jnp.exp(cum_col)

    half = lax.broadcasted_iota(jnp.int32, (1, LANES), 1) < SSD_HEAD_DIM
    nt = (((1,), (1,)), ((), ()))
    for grp in range(N_GROUPS):
        bg = b_mat[:, grp * D_STATE:(grp + 1) * D_STATE]
        cg = c_mat[:, grp * D_STATE:(grp + 1) * D_STATE]
        cb = lax.dot_general(cg.astype(BF16), bg.astype(BF16), nt,
                             preferred_element_type=F32)
        bgt = bg.T
        heads_per_group = SSD_HEADS // N_GROUPS
        for pair in range(grp * heads_per_group // 2, (grp + 1) * heads_per_group // 2):
            lanes = slice(pair * LANES, (pair + 1) * LANES)
            xp = xs_ref[:, lanes]
            zero = jnp.zeros_like(xp)
            x_bd = jnp.concatenate([jnp.where(half, xp, zero), jnp.where(half, zero, xp)],
                                   axis=0)
            sp = s_ref[:, lanes]
            spb = sp.astype(BF16)
            s_bd = jnp.concatenate([jnp.where(half, spb, zero), jnp.where(half, zero, spb)],
                                   axis=0)
            m_parts, c_parts, b_parts = [], [], []
            for h in (2 * pair, 2 * pair + 1):
                colb = cum_col[:, h:h + 1]
                seg = jnp.where(keep, colb - cum_row[h:h + 1, :], -1e30)
                m_parts.append((cb * jnp.exp(seg) * dt_row[h:h + 1, :]).astype(BF16))
                c_parts.append((cg * ecol[:, h:h + 1]).astype(BF16))
                b_parts.append((bgt * wgt_row[h:h + 1, :]).astype(BF16))
            lhs = jnp.concatenate(m_parts + c_parts, axis=1)
            rhs = jnp.concatenate([x_bd, s_bd], axis=0)
            yacc_ref[:, lanes] += _dot(lhs, rhs)
            dec = jnp.where(half, edec[2 * pair:2 * pair + 1, :],
                            edec[2 * pair + 1:2 * pair + 2, :])
            s_ref[:, lanes] = sp * dec + _dot(jnp.concatenate(b_parts, axis=1), x_bd)

    if fwd:
        y = yacc_ref[...] + yb_ref[...].astype(F32)
        yz = y * _silu(z_ref[...].astype(F32))
        ms = jnp.mean(yz * yz, axis=-1, keepdims=True)
        y_ref[...] = (yz * lax.rsqrt(ms + EPS) * nrm_ref[...]).astype(BF16)
    else:
        y_ref[...] = yacc_ref[...].astype(BF16)

    @pl.when(j == nc - 1)
    def _():
        fin_ref[...] = s_ref[...]


def _ssd_sweep(fwd, zero_init, bsz, seqlen, xbc, dt, dtt, alog, cw, cb, init, extra):
    nc = seqlen // CHUNK
    hb = CHUNK // HALO
    n_halo = bsz * seqlen // HALO

    def chunk_of(j):
        return j if fwd else nc - 1 - j

    tok = lambda width: pl.BlockSpec(
        (CHUNK, width), lambda b, j: (b * nc + chunk_of(j), 0))
    prev_spec = pl.BlockSpec(
        (HALO, CONV_CH), lambda b, j: (jnp.maximum((b * nc + chunk_of(j)) * hb - 1, 0), 0))
    next_spec = pl.BlockSpec(
        (HALO, CONV_CH),
        lambda b, j: (jnp.minimum((b * nc + chunk_of(j) + 1) * hb, n_halo - 1), 0))
    dtt_spec = pl.BlockSpec((SSD_HEADS, CHUNK), lambda b, j: (0, b * nc + chunk_of(j)))
    state_spec = pl.BlockSpec((None, D_STATE, SSD_WIDTH), lambda b, j: (b, 0, 0))
    alog_row = jnp.pad(alog.reshape(1, SSD_HEADS), ((0, 0), (0, LANES - SSD_HEADS)))
    alog_col = alog.reshape(SSD_HEADS, 1)
    in_specs = [tok(CONV_CH), prev_spec, next_spec, tok(LANES), dtt_spec,
                _const_spec((1, LANES)), _const_spec((SSD_HEADS, 1)),
                _const_spec(cw.shape), _const_spec(cb.shape)]
    args = [xbc, xbc, xbc, dt, dtt, alog_row, alog_col, cw, cb]
    if not zero_init:
        in_specs.append(state_spec)
        args.append(init)
    if fwd:
        yb, z, dskip, nrm = extra
        in_specs += [tok(SSD_WIDTH), tok(SSD_WIDTH), _const_spec((1, SSD_WIDTH)),
                     _const_spec((1, SSD_WIDTH))]
        args += [yb, z, dskip, nrm]
    return pl.pallas_call(
        functools.partial(_ssd_kernel, fwd, zero_init, nc),
        grid=(bsz, nc),
        in_specs=in_specs,
        out_specs=[tok(SSD_WIDTH), state_spec],
        out_shape=[jax.ShapeDtypeStruct((bsz * seqlen, SSD_WIDTH), BF16),
                   jax.ShapeDtypeStruct((bsz, D_STATE, SSD_WIDTH), F32)],
        scratch_shapes=[pltpu.VMEM((D_STATE, SSD_WIDTH), F32),
                        pltpu.VMEM((CHUNK, SSD_WIDTH), BF16),
                        pltpu.VMEM((CHUNK, SSD_WIDTH), F32)],
        compiler_params=_params(2),
        name="ssd_fwd" if fwd else "ssd_bwd",
    )(*args)


CF_ROWS = 64


def _cf_kernel(tm, tps, u_ref, up_ref, un_ref, w_ref, b_ref, g_ref, beta_ref, o_ref,
               buf_ref, acc_ref):
    t = pl.program_id(0)
    first = (t % tps) == 0
    last = (t % tps) == tps - 1
    buf_ref[0:HALO, :] = jnp.where(first, 0.0, up_ref[...].astype(F32))
    buf_ref[HALO:HALO + tm, :] = u_ref[...].astype(F32)
    buf_ref[HALO + tm:2 * HALO + tm, :] = jnp.where(last, 0.0, un_ref[...].astype(F32))
    off = HALO - (CF_KERNEL - 1) // 2

    def lane_block(jb, carry):
        l0 = pl.multiple_of(jb * LANES, LANES)
        for rb in range(tm // CF_ROWS):
            acc = jnp.zeros((CF_ROWS, LANES), F32) + b_ref[:, pl.ds(l0, LANES)]
            for k in range(CF_KERNEL):
                acc = acc + (w_ref[k:k + 1, pl.ds(l0, LANES)]
                             * buf_ref[rb * CF_ROWS + off + k:(rb + 1) * CF_ROWS + off + k,
                                       pl.ds(l0, LANES)])
            acc_ref[rb * CF_ROWS:(rb + 1) * CF_ROWS, pl.ds(l0, LANES)] = acc
        return carry

    lax.fori_loop(0, CF_WIDTH // LANES, lane_block, 0)
    v = acc_ref[...]
    mu = jnp.mean(v, axis=-1, keepdims=True)
    d = v - mu
    var = jnp.mean(d * d, axis=-1, keepdims=True)
    o_ref[...] = _silu(d * lax.rsqrt(var + EPS) * g_ref[...] + beta_ref[...]).astype(BF16)


def _cf_module(u, seqlen, tm, w, b, g, beta):
    n = u.shape[0]
    tps = seqlen // tm
    hb = tm // HALO
    n_halo = n // HALO
    return pl.pallas_call(
        functools.partial(_cf_kernel, tm, tps),
        grid=(n // tm,),
        in_specs=[pl.BlockSpec((tm, CF_WIDTH), lambda t: (t, 0)),
                  pl.BlockSpec((HALO, CF_WIDTH), lambda t: (jnp.maximum(t * hb - 1, 0), 0)),
                  pl.BlockSpec((HALO, CF_WIDTH),
                               lambda t: (jnp.minimum((t + 1) * hb, n_halo - 1), 0)),
                  _const_spec(w.shape), _const_spec(b.shape), _const_spec(g.shape),
                  _const_spec(beta.shape)],
        out_specs=pl.BlockSpec((tm, CF_WIDTH), lambda t: (t, 0)),
        out_shape=jax.ShapeDtypeStruct((n, CF_WIDTH), BF16),
        scratch_shapes=[pltpu.VMEM((tm + 2 * HALO, CF_WIDTH), F32),
                        pltpu.VMEM((tm, CF_WIDTH), F32)],
        compiler_params=_params(1),
        name="cf_module",
    )(u, u, u, w, b, g, beta)


UP_SPLIT = 4


def _out_proj_kernel(y_ref, u_ref, x_ref, g1_ref, sc_ref, sh_ref, wo1_ref, wo2_ref,
                     gpost_ref, gpre_ref, wup_ref, x1_ref, up_ref):
    mix = _dot(y_ref[...], wo1_ref[...]) + _dot(u_ref[...], wo2_ref[...])
    ms = jnp.mean(mix * mix, axis=-1, keepdims=True)
    x1 = x_ref[...] + g1_ref[...] * (mix * lax.rsqrt(ms + EPS) * gpost_ref[...])
    x1_ref[...] = x1
    ms2 = jnp.mean(x1 * x1, axis=-1, keepdims=True)
    h = x1 * lax.rsqrt(ms2 + EPS) * (gpre_ref[...] * (1.0 + sc_ref[...])) + sh_ref[...]
    hb = h.astype(BF16)
    piece = 2 * D_FF // UP_SPLIT
    for p in range(UP_SPLIT):
        cols = slice(p * piece, (p + 1) * piece)
        up_ref[:, cols] = _dot(hb, wup_ref[:, cols]).astype(BF16)


def _out_proj(y, u, x2d, mod_row, tm, g1, sc, sh, w):
    n = x2d.shape[0]
    mod_spec = pl.BlockSpec((None, 1, D_MODEL), lambda i: (mod_row(i), 0, 0))
    tok = lambda width: pl.BlockSpec((tm, width), lambda i: (i, 0))
    weights = [w["wo1"], w["wo2"], w["gpost1"], w["gpre2"], w["wup"]]
    return pl.pallas_call(
        _out_proj_kernel,
        grid=(n // tm,),
        in_specs=[tok(SSD_WIDTH), tok(CF_WIDTH), tok(D_MODEL), mod_spec, mod_spec, mod_spec]
                 + [_const_spec(a.shape) for a in weights],
        out_specs=[tok(D_MODEL), tok(2 * D_FF)],
        out_shape=[jax.ShapeDtypeStruct((n, D_MODEL), F32),
                   jax.ShapeDtypeStruct((n, 2 * D_FF), BF16)],
        compiler_params=_params(1),
        name="out_proj",
    )(y, u, x2d, g1, sc, sh, *weights)


def _ffn_kernel(grid2d, seg, nseg, tpi, *refs):
    refs = list(refs)
    up_ref = refs.pop(0)
    if grid2d:
        upp_ref, upn_ref = refs[:2]
        del refs[:2]
    x1_ref, g2_ref, cw_ref, cb_ref, wd_ref, gpost_ref, o_ref, act_ref = refs
    t = pl.program_id(0)
    top = (t % tpi) == 0
    bottom = (t % tpi) == tpi - 1
    pos = lax.broadcasted_iota(jnp.int32, (seg, 1), 0)
    has_left = pos > 0
    has_right = pos < seg - 1

    def conv(l0):
        w = cw_ref[:, pl.ds(l0, LANES)]
        bias = cb_ref[:, pl.ds(l0, LANES)]
        rows = [up_ref[s * seg:(s + 1) * seg, pl.ds(l0, LANES)].astype(F32)
                for s in range(nseg)]
        if grid2d:
            above = jnp.where(top, 0.0, upp_ref[:, pl.ds(l0, LANES)].astype(F32))
            below = jnp.where(bottom, 0.0, upn_ref[:, pl.ds(l0, LANES)].astype(F32))
            rows = [above] + rows + [below]
        outs = []
        for s in range(nseg):
            taps = [(rows[s + dr], dr) for dr in range(3)] if grid2d else [(rows[s], 1)]
            v = []
            for dc in range(3):
                acc = None
                for r, dr in taps:
                    term = w[3 * dr + dc:3 * dr + dc + 1, :] * r
                    acc = term if acc is None else acc + term
                v.append(acc)
            left = jnp.where(has_left, pltpu.roll(v[0], 1, axis=0), 0.0)
            right = jnp.where(has_right, pltpu.roll(v[2], seg - 1, axis=0), 0.0)
            outs.append(v[1] + left + right + bias)
        return outs

    def lane_block(jb, carry):
        lg = pl.multiple_of(jb * LANES, LANES)
        lv = pl.multiple_of(D_FF + jb * LANES, LANES)
        gate = conv(lg)
        val = conv(lv)
        for s in range(nseg):
            act_ref[s * seg:(s + 1) * seg, pl.ds(lg, LANES)] = (
                _silu(gate[s]) * val[s]).astype(BF16)
        return carry

    lax.fori_loop(0, D_FF // LANES, lane_block, 0)
    f = _dot(act_ref[...], wd_ref[...])
    ms = jnp.mean(f * f, axis=-1, keepdims=True)
    o_ref[...] = x1_ref[...] + g2_ref[...] * (f * lax.rsqrt(ms + EPS) * gpost_ref[...])


def _ffn(up, x1, mod_row, grid2d, seg, nseg, tpi, g2, cw, cb, wd, gpost):
    n = x1.shape[0]
    tm = seg * nseg
    n_seg_total = n // seg
    mod_spec = pl.BlockSpec((None, 1, D_MODEL), lambda i: (mod_row(i), 0, 0))
    tok = lambda width: pl.BlockSpec((tm, width), lambda i: (i, 0))
    prev_spec = pl.BlockSpec((seg, 2 * D_FF), lambda i: (jnp.maximum(i * nseg - 1, 0), 0))
    next_spec = pl.BlockSpec(
        (seg, 2 * D_FF), lambda i: (jnp.minimum((i + 1) * nseg, n_seg_total - 1), 0))
    halo_specs = [prev_spec, next_spec] if grid2d else []
    halo_args = [up, up] if grid2d else []
    return pl.pallas_call(
        functools.partial(_ffn_kernel, grid2d, seg, nseg, tpi),
        grid=(n // tm,),
        in_specs=[tok(2 * D_FF)] + halo_specs + [
            tok(D_MODEL), mod_spec, _const_spec(cw.shape), _const_spec(cb.shape),
            _const_spec(wd.shape), _const_spec(gpost.shape)],
        out_specs=tok(D_MODEL),
        out_shape=jax.ShapeDtypeStruct((n, D_MODEL), F32),
        scratch_shapes=[pltpu.VMEM((tm, D_FF), BF16)],
        compiler_params=_params(1),
        name="ffn_grid" if grid2d else "ffn_seq",
    )(up, *halo_args, x1, g2, cw, cb, wd, gpost)


def _state_to_kernel_layout(s):
    b = s.shape[0]
    return jnp.transpose(s.reshape(b, SSD_WIDTH, D_STATE), (0, 2, 1))


def _state_from_kernel_layout(s):
    b = s.shape[0]
    return jnp.transpose(s, (0, 2, 1)).reshape(b, SSD_HEADS, SSD_HEAD_DIM, D_STATE)


def _trunk_layer(x, mod, mod_row_of_batch, w, init_f, init_b, rows):
    bsz, seqlen, _ = x.shape
    x2d = x.reshape(bsz * seqlen, D_MODEL)
    sh1, sc1, g1, sh2, sc2, g2 = mod
    latent = rows is not None
    tm = 512 if latent else seqlen
    tps = seqlen // tm
    mod_row = lambda i: mod_row_of_batch(i // tps)

    z, xbc, dtf, dtb, dtft, dtbt, u = _in_proj(x2d, mod_row, tm, sc1, sh1, w["gpre1"], w)

    zero_init = init_f is None
    yb, fin_b = _ssd_sweep(False, zero_init, bsz, seqlen, xbc, dtb, dtbt, w["alog_b"],
                           w["ssd_cw"], w["ssd_cb"], init_b, None)
    y, fin_f = _ssd_sweep(True, zero_init, bsz, seqlen, xbc, dtf, dtft, w["alog_f"],
                          w["ssd_cw"], w["ssd_cb"], init_f,
                          (yb, z, w["dskip"], w["ssd_norm"]))

    ucf = _cf_module(u, seqlen, tm, w["cf_w"], w["cf_b"], w["cf_g"], w["cf_beta"])
    x1, up = _out_proj(y, ucf, x2d, mod_row, tm, g1, sc2, sh2, w)

    if latent:
        nseg = tm // GRID_W
        out = _ffn(up, x1, mod_row, True, GRID_W, nseg, rows // nseg, g2,
                   w["ffn_cw"], w["ffn_cb"], w["wd"], w["gpost2"])
    else:
        out = _ffn(up, x1, mod_row, False, seqlen, 1, 1, g2,
                   w["ffn_cw"], w["ffn_cb"], w["wd"], w["gpost2"])
    return out.reshape(bsz, seqlen, D_MODEL), fin_f, fin_b


def _layer_weights(l, w_in, w_ssd_conv, b_ssd_conv, a_log_fwd, a_log_bwd, dt_bias_fwd,
                   dt_bias_bwd, d_skip, ssd_norm, w_cf_conv, b_cf_conv, cf_ln_g, cf_ln_b,
                   w_out, norm_mix_pre, norm_mix_post, norm_ffn_pre, norm_ffn_post,
                   w_ffn_up, w_ffn_conv, b_ffn_conv, w_ffn_down):
    wi = w_in[l].astype(BF16)
    o = 0
    wz = wi[:, o:o + SSD_WIDTH]; o += SSD_WIDTH
    wx = wi[:, o:o + CONV_CH]; o += CONV_CH
    wdf = wi[:, o:o + SSD_HEADS]; o += SSD_HEADS
    wdb = wi[:, o:o + SSD_HEADS]; o += SSD_HEADS
    wa = wi[:, o:o + CF_WIDTH]; o += CF_WIDTH
    wg = wi[:, o:o + CF_WIDTH]
    pad_l = lambda a: jnp.pad(a, ((0, 0), (0, LANES - a.shape[1])))
    row = lambda a: a.reshape(1, -1)
    wo = w_out[l].astype(BF16)
    cf_w = jnp.pad(w_cf_conv[l], ((0, 32 - CF_KERNEL), (0, 0)))
    ffn_cw = jnp.pad(w_ffn_conv[l].reshape(9, 2 * D_FF), ((0, 7), (0, 0)))
    return {
        "wz": wz, "wx": wx, "wdf": pad_l(wdf), "wdb": pad_l(wdb),
        "wdft": wdf.T, "wdbt": wdb.T, "wa": wa, "wg": wg,
        "bf": pad_l(row(dt_bias_fwd[l])), "bb": pad_l(row(dt_bias_bwd[l])),
        "bft": dt_bias_fwd[l].reshape(-1, 1), "bbt": dt_bias_bwd[l].reshape(-1, 1),
        "gpre1": row(norm_mix_pre[l]), "gpost1": row(norm_mix_post[l]),
        "gpre2": row(norm_ffn_pre[l]), "gpost2": row(norm_ffn_post[l]),
        "alog_f": a_log_fwd[l], "alog_b": a_log_bwd[l],
        "ssd_cw": jnp.pad(w_ssd_conv[l], ((0, 5), (0, 0))), "ssd_cb": row(b_ssd_conv[l]),
        "dskip": row(jnp.repeat(d_skip[l], SSD_HEAD_DIM)), "ssd_norm": row(ssd_norm[l]),
        "cf_w": cf_w, "cf_b": row(b_cf_conv[l]), "cf_g": row(cf_ln_g[l]),
        "cf_beta": row(cf_ln_b[l]),
        "wo1": wo[:SSD_WIDTH], "wo2": wo[SSD_WIDTH:],
        "wup": w_ffn_up[l].astype(BF16), "ffn_cw": ffn_cw, "ffn_cb": row(b_ffn_conv[l]),
        "wd": w_ffn_down[l].astype(BF16),
    }


def kernel(x_prompt, x_sample, state_ssd_fwd, state_ssd_bwd, c, c_ctx, w_ada, b_ada, norm_mix_pre, norm_mix_post, w_in, w_ssd_conv, b_ssd_conv, a_log_fwd, a_log_bwd, dt_bias_fwd, dt_bias_bwd, d_skip, ssd_norm, w_cf_conv, b_cf_conv, cf_ln_g, cf_ln_b, w_out, norm_ffn_pre, norm_ffn_post, w_ffn_up, w_ffn_conv, b_ffn_conv, w_ffn_down):
    depth = w_ada.shape[0]
    dec_batch = x_sample.shape[0]
    rows = x_sample.shape[1] // GRID_W
    ctx_row = dec_batch
    n_cond = -(-(dec_batch + 1) // 8) * 8
    cond = jnp.zeros((n_cond, D_MODEL), F32)
    cond = cond.at[:dec_batch].set(c).at[ctx_row].set(c_ctx)

    xp, xl = x_prompt, x_sample
    new_f, new_b = [], []
    for l in range(depth):
        w = _layer_weights(l, w_in, w_ssd_conv, b_ssd_conv, a_log_fwd, a_log_bwd,
                           dt_bias_fwd, dt_bias_bwd, d_skip, ssd_norm, w_cf_conv, b_cf_conv,
                           cf_ln_g, cf_ln_b, w_out, norm_mix_pre, norm_mix_post,
                           norm_ffn_pre, norm_ffn_post, w_ffn_up, w_ffn_conv, b_ffn_conv,
                           w_ffn_down)
        mod = _modulation(cond, w_ada[l], b_ada[l])
        mod = [m.reshape(n_cond, 1, D_MODEL) for m in jnp.split(mod, 6, axis=-1)]
        xp, s_f, s_b = _trunk_layer(xp, mod, lambda b: ctx_row, w, None, None, None)
        new_f.append(_state_from_kernel_layout(s_f))
        new_b.append(_state_from_kernel_layout(s_b))
        xl, _, _ = _trunk_layer(xl, mod, lambda b: b, w,
                                _state_to_kernel_layout(state_ssd_fwd[:, l]),
                                _state_to_kernel_layout(state_ssd_bwd[:, l]), rows)
    return (xp, xl, jnp.stack(new_f, axis=1), jnp.stack(new_b, axis=1))
```

```python
import functools

import jax
import jax.numpy as jnp
from jax import lax
from jax.experimental import pallas as pl
from jax.experimental.pallas import tpu as pltpu

D_MODEL = 1024
GRID_W = 64
SSD_WIDTH = 1024
SSD_HEAD_DIM = 64
SSD_HEADS = 16
N_GROUPS = 2
D_STATE = 128
CHUNK = 128
CONV_CH = SSD_WIDTH + 2 * N_GROUPS * D_STATE
CF_WIDTH = 1024
CF_KERNEL = 31
D_FF = 2816
EPS = 1e-6

LANES = 128
SUBLANES = 8
HALO = 16
VMEM_LIMIT = 56 * 1024 * 1024

F32 = jnp.float32
BF16 = jnp.bfloat16
HIGHEST = lax.Precision.HIGHEST


def _params(n_axes):
    return pltpu.CompilerParams(
        dimension_semantics=("arbitrary",) * n_axes, vmem_limit_bytes=VMEM_LIMIT)


def _const_spec(shape):
    nd = len(shape)
    return pl.BlockSpec(shape, lambda *_: (0,) * nd, pipeline_mode=pl.Buffered(1))


def _silu(v):
    return v * jax.nn.sigmoid(v)


def _softplus(v):
    return jnp.maximum(v, 0.0) + jnp.log1p(jnp.exp(-jnp.abs(v)))


def _dot(a, b):
    return jnp.dot(a, b, preferred_element_type=F32)


def _mod_kernel(c_ref, w_ref, b_ref, o_ref):
    s = _silu(c_ref[...])
    o_ref[...] = jnp.dot(s, w_ref[...], precision=HIGHEST,
                         preferred_element_type=F32) + b_ref[...]


def _modulation(cond, w_ada, b_ada):
    rows = cond.shape[0]
    n = w_ada.shape[1]
    tn = 1024
    return pl.pallas_call(
        _mod_kernel,
        grid=(n // tn,),
        in_specs=[_const_spec((rows, D_MODEL)),
                  pl.BlockSpec((D_MODEL, tn), lambda j: (0, j)),
                  pl.BlockSpec((1, tn), lambda j: (0, j))],
        out_specs=pl.BlockSpec((rows, tn), lambda j: (0, j)),
        out_shape=jax.ShapeDtypeStruct((rows, n), F32),
        compiler_params=_params(1),
        name="modulation",
    )(cond, w_ada, b_ada.reshape(1, n))


XBC_SPLIT = 3


def _in_proj_kernel(tm, tps, x_ref, xp_ref, xn_ref, sc_ref, sh_ref, g_ref,
                    wz_ref, wx_ref, wdf_ref, wdb_ref, wdft_ref, wdbt_ref, wa_ref, wg_ref,
                    bf_ref, bb_ref, bft_ref, bbt_ref, cw_ref, cb_ref,
                    z_ref, xbc_ref, dtf_ref, dtb_ref, dtft_ref, dtbt_ref, u_ref,
                    hb_ref, *ext_refs):
    t = pl.program_id(0)
    first = (t % tps) == 0
    last = (t % tps) == tps - 1
    scale = g_ref[...] * (1.0 + sc_ref[...])
    shift = sh_ref[...]

    def mod_norm(x):
        ms = jnp.mean(x * x, axis=-1, keepdims=True)
        return (x * lax.rsqrt(ms + EPS) * scale + shift).astype(BF16)

    zero = jnp.zeros((HALO, D_MODEL), BF16)
    hb_ref[0:HALO, :] = jnp.where(first, zero, mod_norm(xp_ref[...]))
    hb_ref[HALO:HALO + tm, :] = mod_norm(x_ref[...])
    hb_ref[HALO + tm:2 * HALO + tm, :] = jnp.where(last, zero, mod_norm(xn_ref[...]))
    hb = hb_ref[HALO:HALO + tm, :]

    piece = CONV_CH // XBC_SPLIT

    def project(p):
        ext_refs[p][...] = _dot(hb_ref[...], wx_ref[:, p * piece:(p + 1) * piece])

    def conv(p):
        cols = slice(p * piece, (p + 1) * piece)
        cw = cw_ref[:, cols]
        ext_ref = ext_refs[p]
        xc = (cw[1:2, :] * ext_ref[HALO:HALO + tm, :] + cb_ref[:, cols]
              + cw[0:1, :] * ext_ref[HALO - 1:HALO - 1 + tm, :]
              + cw[2:3, :] * ext_ref[HALO + 1:HALO + 1 + tm, :])
        xbc_ref[:, cols] = _silu(xc).astype(BF16)

    project(0)
    for p in range(1, XBC_SPLIT):
        project(p)
        conv(p - 1)
    z_ref[...] = _dot(hb, wz_ref[...]).astype(BF16)
    conv(XBC_SPLIT - 1)

    a = _dot(hb, wa_ref[...])
    g = _dot(hb, wg_ref[...])
    u_ref[...] = (a * jax.nn.sigmoid(g)).astype(BF16)
    dtf_ref[...] = _softplus(_dot(hb, wdf_ref[...]) + bf_ref[...])
    dtb_ref[...] = _softplus(_dot(hb, wdb_ref[...]) + bb_ref[...])
    nt = (((1,), (1,)), ((), ()))
    dtft_ref[...] = _softplus(
        lax.dot_general(wdft_ref[...], hb, nt, preferred_element_type=F32) + bft_ref[...])
    dtbt_ref[...] = _softplus(
        lax.dot_general(wdbt_ref[...], hb, nt, preferred_element_type=F32) + bbt_ref[...])


def _in_proj(x2d, seqlen, mod_row, tm, sc, sh, g, w):
    n = x2d.shape[0]
    hb = tm // HALO
    n_halo = n // HALO
    mod_spec = pl.BlockSpec((None, 1, D_MODEL), lambda i: (mod_row(i), 0, 0))
    tok = lambda width: pl.BlockSpec((tm, width), lambda i: (i, 0))
    tok_t = pl.BlockSpec((SSD_HEADS, tm), lambda i: (0, i))
    prev_spec = pl.BlockSpec((HALO, D_MODEL), lambda i: (jnp.maximum(i * hb - 1, 0), 0))
    next_spec = pl.BlockSpec((HALO, D_MODEL),
                             lambda i: (jnp.minimum((i + 1) * hb, n_halo - 1), 0))
    weights = [w["wz"], w["wx"], w["wdf"], w["wdb"], w["wdft"], w["wdbt"], w["wa"], w["wg"],
               w["bf"], w["bb"], w["bft"], w["bbt"], w["ssd_cw"], w["ssd_cb"]]
    return pl.pallas_call(
        functools.partial(_in_proj_kernel, tm, seqlen // tm),
        grid=(n // tm,),
        in_specs=[tok(D_MODEL), prev_spec, next_spec, mod_spec, mod_spec,
                  _const_spec((1, D_MODEL))]
                 + [_const_spec(a.shape) for a in weights],
        scratch_shapes=[pltpu.VMEM((tm + 2 * HALO, D_MODEL), BF16)]
                       + [pltpu.VMEM((tm + 2 * HALO, CONV_CH // XBC_SPLIT), F32)] * XBC_SPLIT,
        out_specs=[tok(SSD_WIDTH), tok(CONV_CH), tok(LANES), tok(LANES), tok_t, tok_t,
                   tok(CF_WIDTH)],
        out_shape=[jax.ShapeDtypeStruct((n, SSD_WIDTH), BF16),
                   jax.ShapeDtypeStruct((n, CONV_CH), BF16),
                   jax.ShapeDtypeStruct((n, LANES), F32),
                   jax.ShapeDtypeStruct((n, LANES), F32),
                   jax.ShapeDtypeStruct((SSD_HEADS, n), F32),
                   jax.ShapeDtypeStruct((SSD_HEADS, n), F32),
                   jax.ShapeDtypeStruct((n, CF_WIDTH), BF16)],
        compiler_params=_params(1),
        name="in_proj",
    )(x2d, x2d, x2d, sc, sh, g, *weights)


LOG2E = 1.4426950408889634
N_PAIRS = SSD_HEADS // 2
SSD_CHUNKS_PER_STEP = 4


def _ssd_kernel(fwd, zero_init, cps, *refs):
    refs = list(refs)
    xbc_ref, dt_ref, dtt_ref, alr_ref, alc_ref = refs[:5]
    del refs[:5]
    init_ref = None if zero_init else refs.pop(0)
    if fwd:
        yb_ref, z_ref, dskip_ref, nrm_ref = refs[:4]
        del refs[:4]
    y_ref, fin_ref = refs[:2]
    s_refs = refs[2:2 + N_PAIRS]
    yacc_refs = refs[2 + N_PAIRS:]
    j = pl.program_id(1)

    @pl.when(j == 0)
    def _():
        for pair in range(N_PAIRS):
            if zero_init:
                s_refs[pair][...] = jnp.zeros((D_STATE, LANES), F32)
            else:
                s_refs[pair][...] = init_ref[:, pair * LANES:(pair + 1) * LANES]

    ri = lax.broadcasted_iota(jnp.int32, (CHUNK, CHUNK), 0)
    ci = lax.broadcasted_iota(jnp.int32, (CHUNK, CHUNK), 1)
    keep = (ri >= ci) if fwd else (ri <= ci)
    tri = keep.astype(F32)
    lane = lax.broadcasted_iota(jnp.int32, (1, LANES), 1)
    a_row = jnp.where(lane < SSD_HEADS, -jnp.exp(alr_ref[...]) * LOG2E, 0.0)
    a_col = -jnp.exp(alc_ref[...]) * LOG2E
    half = lane < SSD_HEAD_DIM
    nt = (((1,), (1,)), ((), ()))
    heads_per_group = SSD_HEADS // N_GROUPS
    end = CHUNK - 1 if fwd else 0

    pre = []
    for sc in range(cps):
        rows = slice(sc * CHUNK, (sc + 1) * CHUNK)
        dt_row = dtt_ref[:, rows]
        cum_col = jnp.dot(tri, dt_ref[rows, :] * a_row, precision=HIGHEST,
                          preferred_element_type=F32)
        cum_row = lax.dot_general(dt_row * a_col, tri, nt, precision=HIGHEST,
                                  preferred_element_type=F32)
        cum_end = cum_row[:, end:end + 1]
        p = {
            "cum_col": cum_col,
            "wgt_row": jnp.exp2(cum_end - cum_row) * dt_row,
            "edec": jnp.exp2(cum_end),
            "ecol": jnp.exp2(cum_col),
            "src_row": cum_row - jnp.log2(dt_row),
            "cb": [], "cg": [], "bgt": [],
        }
        for grp in range(N_GROUPS):
            b_cols = slice(SSD_WIDTH + grp * D_STATE, SSD_WIDTH + (grp + 1) * D_STATE)
            c_cols = slice(SSD_WIDTH + (N_GROUPS + grp) * D_STATE,
                           SSD_WIDTH + (N_GROUPS + grp + 1) * D_STATE)
            p["cb"].append(lax.dot_general(xbc_ref[rows, c_cols], xbc_ref[rows, b_cols], nt,
                                           preferred_element_type=F32))
            p["cg"].append(xbc_ref[rows, c_cols].astype(F32))
            p["bgt"].append(xbc_ref[rows, b_cols].astype(F32).T)
        pre.append(p)

    for sc in (range(cps) if fwd else reversed(range(cps))):
        rows = slice(sc * CHUNK, (sc + 1) * CHUNK)
        p = pre[sc]
        cum_col, wgt_row, edec, ecol, src_row = (
            p["cum_col"], p["wgt_row"], p["edec"], p["ecol"], p["src_row"])
        for grp in range(N_GROUPS):
            cb, cg, bgt = p["cb"][grp], p["cg"][grp], p["bgt"][grp]
            for pair in range(grp * heads_per_group // 2, (grp + 1) * heads_per_group // 2):
                lanes = slice(pair * LANES, (pair + 1) * LANES)
                xp = xbc_ref[rows, lanes]
                zero = jnp.zeros_like(xp)
                x_bd = jnp.concatenate(
                    [jnp.where(half, xp, zero), jnp.where(half, zero, xp)], axis=0)
                sp = s_refs[pair][...]
                spb = sp.astype(BF16)
                s_bd = jnp.concatenate(
                    [jnp.where(half, spb, zero), jnp.where(half, zero, spb)], axis=0)
                m_parts, c_parts, b_parts = [], [], []
                for h in (2 * pair, 2 * pair + 1):
                    seg = jnp.where(keep, cum_col[:, h:h + 1] - src_row[h:h + 1, :], -1e30)
                    m_parts.append((cb * jnp.exp2(seg)).astype(BF16))
                    c_parts.append((cg * ecol[:, h:h + 1]).astype(BF16))
                    b_parts.append((bgt * wgt_row[h:h + 1, :]).astype(BF16))
                lhs = jnp.concatenate(m_parts + c_parts, axis=1)
                rhs = jnp.concatenate([x_bd, s_bd], axis=0)
                y_pair = _dot(lhs, rhs)
                if fwd:
                    yacc_refs[sc * N_PAIRS + pair][...] = (
                        y_pair + xp.astype(F32) * dskip_ref[:, lanes])
                else:
                    y_ref[rows, lanes] = y_pair.astype(BF16)
                dec = jnp.where(half, edec[2 * pair:2 * pair + 1, :],
                                edec[2 * pair + 1:2 * pair + 2, :])
                s_refs[pair][...] = sp * dec + _dot(jnp.concatenate(b_parts, axis=1), x_bd)

        if fwd:
            yz = []
            for pair in range(N_PAIRS):
                lanes = slice(pair * LANES, (pair + 1) * LANES)
                y = yacc_refs[sc * N_PAIRS + pair][...] + yb_ref[rows, lanes].astype(F32)
                yz.append(y * _silu(z_ref[rows, lanes].astype(F32)))
            sq = yz[0] * yz[0]
            for v in yz[1:]:
                sq = sq + v * v
            rstd = lax.rsqrt(jnp.sum(sq, axis=-1, keepdims=True) * (1.0 / SSD_WIDTH) + EPS)
            for pair in range(N_PAIRS):
                lanes = slice(pair * LANES, (pair + 1) * LANES)
                y_ref[rows, lanes] = (yz[pair] * rstd * nrm_ref[:, lanes]).astype(BF16)

    @pl.when(j == pl.num_programs(1) - 1)
    def _():
        for pair in range(N_PAIRS):
            fin_ref[:, pair * LANES:(pair + 1) * LANES] = s_refs[pair][...]


def _ssd_sweep(fwd, zero_init, bsz, seqlen, xbc, dt, dtt, alog, init, extra):
    cps = min(SSD_CHUNKS_PER_STEP, seqlen // CHUNK)
    blk = cps * CHUNK
    ns = seqlen // blk

    def block_of(j):
        return j if fwd else ns - 1 - j

    tok = lambda width: pl.BlockSpec(
        (blk, width), lambda b, j: (b * ns + block_of(j), 0))
    dtt_spec = pl.BlockSpec((SSD_HEADS, blk), lambda b, j: (0, b * ns + block_of(j)))
    state_spec = pl.BlockSpec((None, D_STATE, SSD_WIDTH), lambda b, j: (b, 0, 0))
    alog_row = jnp.pad(alog.reshape(1, SSD_HEADS), ((0, 0), (0, LANES - SSD_HEADS)))
    alog_col = alog.reshape(SSD_HEADS, 1)
    in_specs = [tok(CONV_CH), tok(LANES), dtt_spec,
                _const_spec((1, LANES)), _const_spec((SSD_HEADS, 1))]
    args = [xbc, dt, dtt, alog_row, alog_col]
    if not zero_init:
        in_specs.append(state_spec)
        args.append(init)
    if fwd:
        yb, z, dskip, nrm = extra
        in_specs += [tok(SSD_WIDTH), tok(SSD_WIDTH), _const_spec((1, SSD_WIDTH)),
                     _const_spec((1, SSD_WIDTH))]
        args += [yb, z, dskip, nrm]
    return pl.pallas_call(
        functools.partial(_ssd_kernel, fwd, zero_init, cps),
        grid=(bsz, ns),
        in_specs=in_specs,
        out_specs=[tok(SSD_WIDTH), state_spec],
        out_shape=[jax.ShapeDtypeStruct((bsz * seqlen, SSD_WIDTH), BF16),
                   jax.ShapeDtypeStruct((bsz, D_STATE, SSD_WIDTH), F32)],
        scratch_shapes=[pltpu.VMEM((D_STATE, LANES), F32)] * N_PAIRS
                       + [pltpu.VMEM((CHUNK, LANES), F32)] * (cps * N_PAIRS if fwd else 0),
        compiler_params=_params(2),
        name="ssd_fwd" if fwd else "ssd_bwd",
    )(*args)


def _cf_pitches(tm):
    seg = tm // SUBLANES
    in_pitch = -(-(seg + CF_KERNEL - 1 - 4) // 8) * 8 + 4
    out_pitch = seg + 8
    return seg, in_pitch, out_pitch


def _cf_kernel(tm, tps, u_ref, up_ref, un_ref, w_ref, b_ref, g_ref, beta_ref, o_ref,
               buf_ref, slab_ref, acc_ref):
    seg, in_pitch, out_pitch = _cf_pitches(tm)
    pad = (CF_KERNEL - 1) // 2
    t = pl.program_id(0)
    first = (t % tps) == 0
    last = (t % tps) == tps - 1
    buf_ref[0:HALO, :] = jnp.where(first, 0.0, up_ref[...].astype(F32))
    buf_ref[HALO:HALO + tm, :] = u_ref[...].astype(F32)
    buf_ref[HALO + tm:2 * HALO + tm, :] = jnp.where(last, 0.0, un_ref[...].astype(F32))
    span = seg + 2 * pad
    for jb in range(CF_WIDTH // LANES):
        for r in range(SUBLANES):
            src = HALO - pad + r * seg
            slab_ref[jb, r * in_pitch:r * in_pitch + span, :] = (
                buf_ref[src:src + span, jb * LANES:(jb + 1) * LANES])

    def lane_block(jb, carry):
        l0 = pl.multiple_of(jb * LANES, LANES)
        w = w_ref[:, pl.ds(l0, LANES)]
        taps = [jnp.broadcast_to(w[k:k + 1, :], (SUBLANES, LANES)) for k in range(CF_KERNEL)]
        bias = jnp.broadcast_to(b_ref[:, pl.ds(l0, LANES)], (SUBLANES, LANES))
        for i in range(seg):
            acc = bias
            for k in range(CF_KERNEL):
                acc = acc + taps[k] * slab_ref[jb, pl.ds(i + k, SUBLANES, stride=in_pitch), :]
            acc_ref[jb, pl.ds(i, SUBLANES, stride=out_pitch), :] = acc
        return carry

    lax.fori_loop(0, CF_WIDTH // LANES, lane_block, 0)

    nb = CF_WIDTH // LANES
    for r in range(SUBLANES):
        v = [acc_ref[jb, r * out_pitch:r * out_pitch + seg, :] for jb in range(nb)]
        mu = jnp.sum(sum(v[1:], v[0]), axis=-1, keepdims=True) * (1.0 / CF_WIDTH)
        d = [vj - mu for vj in v]
        sq = d[0] * d[0]
        for dj in d[1:]:
            sq = sq + dj * dj
        rstd = lax.rsqrt(jnp.sum(sq, axis=-1, keepdims=True) * (1.0 / CF_WIDTH) + EPS)
        for jb in range(nb):
            lanes = slice(jb * LANES, (jb + 1) * LANES)
            y = d[jb] * rstd * g_ref[:, lanes] + beta_ref[:, lanes]
            o_ref[r * seg:(r + 1) * seg, lanes] = _silu(y).astype(BF16)


def _cf_module(u, seqlen, tm, w, b, g, beta):
    n = u.shape[0]
    tps = seqlen // tm
    hb = tm // HALO
    n_halo = n // HALO
    _, in_pitch, out_pitch = _cf_pitches(tm)
    nb = CF_WIDTH // LANES
    return pl.pallas_call(
        functools.partial(_cf_kernel, tm, tps),
        grid=(n // tm,),
        in_specs=[pl.BlockSpec((tm, CF_WIDTH), lambda t: (t, 0)),
                  pl.BlockSpec((HALO, CF_WIDTH), lambda t: (jnp.maximum(t * hb - 1, 0), 0)),
                  pl.BlockSpec((HALO, CF_WIDTH),
                               lambda t: (jnp.minimum((t + 1) * hb, n_halo - 1), 0)),
                  _const_spec(w.shape), _const_spec(b.shape), _const_spec(g.shape),
                  _const_spec(beta.shape)],
        out_specs=pl.BlockSpec((tm, CF_WIDTH), lambda t: (t, 0)),
        out_shape=jax.ShapeDtypeStruct((n, CF_WIDTH), BF16),
        scratch_shapes=[pltpu.VMEM((tm + 2 * HALO, CF_WIDTH), F32),
                        pltpu.VMEM((nb, SUBLANES * in_pitch, LANES), F32),
                        pltpu.VMEM((nb, SUBLANES * out_pitch, LANES), F32)],
        compiler_params=_params(1),
        name="cf_module",
    )(u, u, u, w, b, g, beta)


UP_SPLIT = 4


def _out_proj_kernel(y_ref, u_ref, x_ref, g1_ref, sc_ref, sh_ref, wo1_ref, wo2_ref,
                     gpost_ref, gpre_ref, wup_ref, x1_ref, up_ref):
    mix = _dot(y_ref[...], wo1_ref[...]) + _dot(u_ref[...], wo2_ref[...])
    ms = jnp.mean(mix * mix, axis=-1, keepdims=True)
    x1 = x_ref[...] + g1_ref[...] * (mix * lax.rsqrt(ms + EPS) * gpost_ref[...])
    x1_ref[...] = x1
    ms2 = jnp.mean(x1 * x1, axis=-1, keepdims=True)
    h = x1 * lax.rsqrt(ms2 + EPS) * (gpre_ref[...] * (1.0 + sc_ref[...])) + sh_ref[...]
    hb = h.astype(BF16)
    piece = 2 * D_FF // UP_SPLIT
    for p in range(UP_SPLIT):
        cols = slice(p * piece, (p + 1) * piece)
        up_ref[:, cols] = _dot(hb, wup_ref[:, cols]).astype(BF16)


def _out_proj(y, u, x2d, mod_row, tm, g1, sc, sh, w):
    n = x2d.shape[0]
    mod_spec = pl.BlockSpec((None, 1, D_MODEL), lambda i: (mod_row(i), 0, 0))
    tok = lambda width: pl.BlockSpec((tm, width), lambda i: (i, 0))
    weights = [w["wo1"], w["wo2"], w["gpost1"], w["gpre2"], w["wup"]]
    return pl.pallas_call(
        _out_proj_kernel,
        grid=(n // tm,),
        in_specs=[tok(SSD_WIDTH), tok(CF_WIDTH), tok(D_MODEL), mod_spec, mod_spec, mod_spec]
                 + [_const_spec(a.shape) for a in weights],
        out_specs=[tok(D_MODEL), tok(2 * D_FF)],
        out_shape=[jax.ShapeDtypeStruct((n, D_MODEL), F32),
                   jax.ShapeDtypeStruct((n, 2 * D_FF), BF16)],
        compiler_params=_params(1),
        name="out_proj",
    )(y, u, x2d, g1, sc, sh, *weights)


def _ffn_kernel(grid2d, seg, nseg, tpi, *refs):
    refs = list(refs)
    up_ref = refs.pop(0)
    if grid2d:
        upp_ref, upn_ref = refs[:2]
        del refs[:2]
    x1_ref, g2_ref, cw_ref, cb_ref, wd_ref, gpost_ref, o_ref, act_ref = refs
    t = pl.program_id(0)
    top = (t % tpi) == 0
    bottom = (t % tpi) == tpi - 1
    pos = lax.broadcasted_iota(jnp.int32, (seg, 1), 0)
    has_left = pos > 0
    has_right = pos < seg - 1

    def row(s, l0):
        return up_ref[s * seg:(s + 1) * seg, pl.ds(l0, LANES)].astype(F32)

    def halo(ref, at_edge, l0):
        return jnp.where(at_edge, 0.0, ref[:, pl.ds(l0, LANES)].astype(F32))

    def conv_seg(taps, w, bias):
        v = []
        for dc in range(3):
            acc = None
            for r, dr in taps:
                term = w[3 * dr + dc:3 * dr + dc + 1, :] * r
                acc = term if acc is None else acc + term
            v.append(acc)
        left = jnp.where(has_left, pltpu.roll(v[0], 1, axis=0), 0.0)
        right = jnp.where(has_right, pltpu.roll(v[2], seg - 1, axis=0), 0.0)
        return v[1] + left + right + bias

    def lane_block(jb, carry):
        lg = pl.multiple_of(jb * LANES, LANES)
        lanes = [lg, pl.multiple_of(D_FF + jb * LANES, LANES)]
        w = [cw_ref[:, pl.ds(l0, LANES)] for l0 in lanes]
        bias = [cb_ref[:, pl.ds(l0, LANES)] for l0 in lanes]
        if grid2d:
            win = [[halo(upp_ref, top, l0), row(0, l0)] for l0 in lanes]
        for s in range(nseg):
            outs = []
            for i, l0 in enumerate(lanes):
                if grid2d:
                    nxt = row(s + 1, l0) if s + 1 < nseg else halo(upn_ref, bottom, l0)
                    rows = win[i] + [nxt]
                    win[i] = rows[1:]
                    taps = [(rows[dr], dr) for dr in range(3)]
                else:
                    taps = [(row(s, l0), 1)]
                outs.append(conv_seg(taps, w[i], bias[i]))
            act_ref[s * seg:(s + 1) * seg, pl.ds(lg, LANES)] = (
                _silu(outs[0]) * outs[1]).astype(BF16)
        return carry

    lax.fori_loop(0, D_FF // LANES, lane_block, 0)
    f = _dot(act_ref[...], wd_ref[...])
    ms = jnp.mean(f * f, axis=-1, keepdims=True)
    o_ref[...] = x1_ref[...] + g2_ref[...] * (f * lax.rsqrt(ms + EPS) * gpost_ref[...])


def _ffn(up, x1, mod_row, grid2d, seg, nseg, tpi, g2, cw, cb, wd, gpost):
    n = x1.shape[0]
    tm = seg * nseg
    n_seg_total = n // seg
    mod_spec = pl.BlockSpec((None, 1, D_MODEL), lambda i: (mod_row(i), 0, 0))
    tok = lambda width: pl.BlockSpec((tm, width), lambda i: (i, 0))
    prev_spec = pl.BlockSpec((seg, 2 * D_FF), lambda i: (jnp.maximum(i * nseg - 1, 0), 0))
    next_spec = pl.BlockSpec(
        (seg, 2 * D_FF), lambda i: (jnp.minimum((i + 1) * nseg, n_seg_total - 1), 0))
    halo_specs = [prev_spec, next_spec] if grid2d else []
    halo_args = [up, up] if grid2d else []
    return pl.pallas_call(
        functools.partial(_ffn_kernel, grid2d, seg, nseg, tpi),
        grid=(n // tm,),
        in_specs=[tok(2 * D_FF)] + halo_specs + [
            tok(D_MODEL), mod_spec, _const_spec(cw.shape), _const_spec(cb.shape),
            _const_spec(wd.shape), _const_spec(gpost.shape)],
        out_specs=tok(D_MODEL),
        out_shape=jax.ShapeDtypeStruct((n, D_MODEL), F32),
        scratch_shapes=[pltpu.VMEM((tm, D_FF), BF16)],
        compiler_params=_params(1),
        name="ffn_grid" if grid2d else "ffn_seq",
    )(up, *halo_args, x1, g2, cw, cb, wd, gpost)


def _state_to_kernel_layout(s):
    b = s.shape[0]
    return jnp.transpose(s.reshape(b, SSD_WIDTH, D_STATE), (0, 2, 1))


def _state_from_kernel_layout(s):
    b = s.shape[0]
    return jnp.transpose(s, (0, 2, 1)).reshape(b, SSD_HEADS, SSD_HEAD_DIM, D_STATE)


def _trunk_layer(x, mod, mod_row_of_batch, w, init_f, init_b, rows):
    bsz, seqlen, _ = x.shape
    x2d = x.reshape(bsz * seqlen, D_MODEL)
    sh1, sc1, g1, sh2, sc2, g2 = mod
    latent = rows is not None
    tm = 512 if latent else seqlen
    tps = seqlen // tm
    mod_row = lambda i: mod_row_of_batch(i // tps)

    z, xbc, dtf, dtb, dtft, dtbt, u = _in_proj(x2d, seqlen, mod_row, tm, sc1, sh1,
                                               w["gpre1"], w)

    zero_init = init_f is None
    yb, fin_b = _ssd_sweep(False, zero_init, bsz, seqlen, xbc, dtb, dtbt, w["alog_b"],
                           init_b, None)
    y, fin_f = _ssd_sweep(True, zero_init, bsz, seqlen, xbc, dtf, dtft, w["alog_f"],
                          init_f, (yb, z, w["dskip"], w["ssd_norm"]))

    ucf = _cf_module(u, seqlen, tm, w["cf_w"], w["cf_b"], w["cf_g"], w["cf_beta"])
    x1, up = _out_proj(y, ucf, x2d, mod_row, tm, g1, sc2, sh2, w)

    if latent:
        nseg = tm // GRID_W
        out = _ffn(up, x1, mod_row, True, GRID_W, nseg, rows // nseg, g2,
                   w["ffn_cw"], w["ffn_cb"], w["wd"], w["gpost2"])
    else:
        out = _ffn(up, x1, mod_row, False, seqlen, 1, 1, g2,
                   w["ffn_cw"], w["ffn_cb"], w["wd"], w["gpost2"])
    return out.reshape(bsz, seqlen, D_MODEL), fin_f, fin_b


def _layer_weights(l, w_in, w_ssd_conv, b_ssd_conv, a_log_fwd, a_log_bwd, dt_bias_fwd,
                   dt_bias_bwd, d_skip, ssd_norm, w_cf_conv, b_cf_conv, cf_ln_g, cf_ln_b,
                   w_out, norm_mix_pre, norm_mix_post, norm_ffn_pre, norm_ffn_post,
                   w_ffn_up, w_ffn_conv, b_ffn_conv, w_ffn_down):
    wi = w_in[l].astype(BF16)
    o = 0
    wz = wi[:, o:o + SSD_WIDTH]; o += SSD_WIDTH
    wx = wi[:, o:o + CONV_CH]; o += CONV_CH
    wdf = wi[:, o:o + SSD_HEADS]; o += SSD_HEADS
    wdb = wi[:, o:o + SSD_HEADS]; o += SSD_HEADS
    wa = wi[:, o:o + CF_WIDTH]; o += CF_WIDTH
    wg = wi[:, o:o + CF_WIDTH]
    pad_l = lambda a: jnp.pad(a, ((0, 0), (0, LANES - a.shape[1])))
    row = lambda a: a.reshape(1, -1)
    wo = w_out[l].astype(BF16)
    cf_w = jnp.pad(w_cf_conv[l], ((0, 32 - CF_KERNEL), (0, 0)))
    ffn_cw = jnp.pad(w_ffn_conv[l].reshape(9, 2 * D_FF), ((0, 7), (0, 0)))
    return {
        "wz": wz, "wx": wx, "wdf": pad_l(wdf), "wdb": pad_l(wdb),
        "wdft": wdf.T, "wdbt": wdb.T, "wa": wa, "wg": wg,
        "bf": pad_l(row(dt_bias_fwd[l])), "bb": pad_l(row(dt_bias_bwd[l])),
        "bft": dt_bias_fwd[l].reshape(-1, 1), "bbt": dt_bias_bwd[l].reshape(-1, 1),
        "gpre1": row(norm_mix_pre[l]), "gpost1": row(norm_mix_post[l]),
        "gpre2": row(norm_ffn_pre[l]), "gpost2": row(norm_ffn_post[l]),
        "alog_f": a_log_fwd[l], "alog_b": a_log_bwd[l],
        "ssd_cw": jnp.pad(w_ssd_conv[l], ((0, 5), (0, 0))), "ssd_cb": row(b_ssd_conv[l]),
        "dskip": row(jnp.repeat(d_skip[l], SSD_HEAD_DIM)), "ssd_norm": row(ssd_norm[l]),
        "cf_w": cf_w, "cf_b": row(b_cf_conv[l]), "cf_g": row(cf_ln_g[l]),
        "cf_beta": row(cf_ln_b[l]),
        "wo1": wo[:SSD_WIDTH], "wo2": wo[SSD_WIDTH:],
        "wup": w_ffn_up[l].astype(BF16), "ffn_cw": ffn_cw, "ffn_cb": row(b_ffn_conv[l]),
        "wd": w_ffn_down[l].astype(BF16),
    }


def kernel(x_prompt, x_sample, state_ssd_fwd, state_ssd_bwd, c, c_ctx, w_ada, b_ada, norm_mix_pre, norm_mix_post, w_in, w_ssd_conv, b_ssd_conv, a_log_fwd, a_log_bwd, dt_bias_fwd, dt_bias_bwd, d_skip, ssd_norm, w_cf_conv, b_cf_conv, cf_ln_g, cf_ln_b, w_out, norm_ffn_pre, norm_ffn_post, w_ffn_up, w_ffn_conv, b_ffn_conv, w_ffn_down):
    depth = w_ada.shape[0]
    dec_batch = x_sample.shape[0]
    rows = x_sample.shape[1] // GRID_W
    ctx_row = dec_batch
    n_cond = -(-(dec_batch + 1) // 8) * 8
    cond = jnp.zeros((n_cond, D_MODEL), F32)
    cond = cond.at[:dec_batch].set(c).at[ctx_row].set(c_ctx)

    xp, xl = x_prompt, x_sample
    new_f, new_b = [], []
    for l in range(depth):
        w = _layer_weights(l, w_in, w_ssd_conv, b_ssd_conv, a_log_fwd, a_log_bwd,
                           dt_bias_fwd, dt_bias_bwd, d_skip, ssd_norm, w_cf_conv, b_cf_conv,
                           cf_ln_g, cf_ln_b, w_out, norm_mix_pre, norm_mix_post,
                           norm_ffn_pre, norm_ffn_post, w_ffn_up, w_ffn_conv, b_ffn_conv,
                           w_ffn_down)
        mod = _modulation(cond, w_ada[l], b_ada[l])
        mod = [m.reshape(n_cond, 1, D_MODEL) for m in jnp.split(mod, 6, axis=-1)]
        xp, s_f, s_b = _trunk_layer(xp, mod, lambda b: ctx_row, w, None, None, None)
        new_f.append(_state_from_kernel_layout(s_f))
        new_b.append(_state_from_kernel_layout(s_b))
        xl, _, _ = _trunk_layer(xl, mod, lambda b: b, w,
                                _state_to_kernel_layout(state_ssd_fwd[:, l]),
                                _state_to_kernel_layout(state_ssd_bwd[:, l]), rows)
    return (xp, xl, jnp.stack(new_f, axis=1), jnp.stack(new_b, axis=1))
```

```python
import functools

import jax
import jax.numpy as jnp
from jax import lax
from jax.experimental import pallas as pl
from jax.experimental.pallas import tpu as pltpu

D_MODEL = 1024
GRID_W = 64
SSD_WIDTH = 1024
SSD_HEAD_DIM = 64
SSD_HEADS = 16
N_GROUPS = 2
D_STATE = 128
CHUNK = 128
CONV_CH = SSD_WIDTH + 2 * N_GROUPS * D_STATE
CF_WIDTH = 1024
CF_KERNEL = 31
D_FF = 2816
EPS = 1e-6

LANES = 128
SUBLANES = 8
HALO = 16
VMEM_LIMIT = 56 * 1024 * 1024

F32 = jnp.float32
BF16 = jnp.bfloat16
HIGHEST = lax.Precision.HIGHEST


def _params(n_axes):
    return pltpu.CompilerParams(
        dimension_semantics=("arbitrary",) * n_axes, vmem_limit_bytes=VMEM_LIMIT)


def _const_spec(shape):
    nd = len(shape)
    return pl.BlockSpec(shape, lambda *_: (0,) * nd, pipeline_mode=pl.Buffered(1))


def _silu(v):
    return v * jax.nn.sigmoid(v)


def _softplus(v):
    return jnp.maximum(v, 0.0) + jnp.log1p(jnp.exp(-jnp.abs(v)))


def _dot(a, b):
    return jnp.dot(a, b, preferred_element_type=F32)


def _mod_kernel(c_ref, w_ref, b_ref, o_ref):
    s = _silu(c_ref[...])
    o_ref[...] = jnp.dot(s, w_ref[...], precision=HIGHEST,
                         preferred_element_type=F32) + b_ref[...]


def _modulation(cond, w_ada, b_ada):
    rows = cond.shape[0]
    n = w_ada.shape[1]
    tn = 1024
    return pl.pallas_call(
        _mod_kernel,
        grid=(n // tn,),
        in_specs=[_const_spec((rows, D_MODEL)),
                  pl.BlockSpec((D_MODEL, tn), lambda j: (0, j)),
                  pl.BlockSpec((1, tn), lambda j: (0, j))],
        out_specs=pl.BlockSpec((rows, tn), lambda j: (0, j)),
        out_shape=jax.ShapeDtypeStruct((rows, n), F32),
        compiler_params=_params(1),
        name="modulation",
    )(cond, w_ada, b_ada.reshape(1, n))


XBC_SPLIT = 3


def _in_proj_kernel(tm, tps, x_ref, xp_ref, xn_ref, sc_ref, sh_ref, g_ref,
                    wz_ref, wx_ref, wdf_ref, wdb_ref, wdft_ref, wdbt_ref, wa_ref, wg_ref,
                    bf_ref, bb_ref, bft_ref, bbt_ref, cw_ref, cb_ref,
                    z_ref, xbc_ref, dtf_ref, dtb_ref, dtft_ref, dtbt_ref, u_ref,
                    hb_ref, *ext_refs):
    t = pl.program_id(0)
    first = (t % tps) == 0
    last = (t % tps) == tps - 1
    scale = g_ref[...] * (1.0 + sc_ref[...])
    shift = sh_ref[...]

    def mod_norm(x):
        ms = jnp.mean(x * x, axis=-1, keepdims=True)
        return (x * lax.rsqrt(ms + EPS) * scale + shift).astype(BF16)

    zero = jnp.zeros((HALO, D_MODEL), BF16)
    hb_ref[0:HALO, :] = jnp.where(first, zero, mod_norm(xp_ref[...]))
    hb_ref[HALO:HALO + tm, :] = mod_norm(x_ref[...])
    hb_ref[HALO + tm:2 * HALO + tm, :] = jnp.where(last, zero, mod_norm(xn_ref[...]))
    hb = hb_ref[HALO:HALO + tm, :]

    piece = CONV_CH // XBC_SPLIT

    def project(p):
        ext_refs[p][...] = _dot(hb_ref[...], wx_ref[:, p * piece:(p + 1) * piece])

    def conv(p):
        cols = slice(p * piece, (p + 1) * piece)
        cw = cw_ref[:, cols]
        ext_ref = ext_refs[p]
        xc = (cw[1:2, :] * ext_ref[HALO:HALO + tm, :] + cb_ref[:, cols]
              + cw[0:1, :] * ext_ref[HALO - 1:HALO - 1 + tm, :]
              + cw[2:3, :] * ext_ref[HALO + 1:HALO + 1 + tm, :])
        xbc_ref[:, cols] = _silu(xc).astype(BF16)

    project(0)
    for p in range(1, XBC_SPLIT):
        project(p)
        conv(p - 1)
    z_ref[...] = _dot(hb, wz_ref[...]).astype(BF16)
    conv(XBC_SPLIT - 1)

    a = _dot(hb, wa_ref[...])
    g = _dot(hb, wg_ref[...])
    u_ref[...] = (a * jax.nn.sigmoid(g)).astype(BF16)
    dtf_ref[...] = _softplus(_dot(hb, wdf_ref[...]) + bf_ref[...])
    dtb_ref[...] = _softplus(_dot(hb, wdb_ref[...]) + bb_ref[...])
    nt = (((1,), (1,)), ((), ()))
    dtft_ref[...] = _softplus(
        lax.dot_general(wdft_ref[...], hb, nt, preferred_element_type=F32) + bft_ref[...])
    dtbt_ref[...] = _softplus(
        lax.dot_general(wdbt_ref[...], hb, nt, preferred_element_type=F32) + bbt_ref[...])


def _in_proj(x2d, seqlen, mod_row, tm, sc, sh, g, w):
    n = x2d.shape[0]
    hb = tm // HALO
    n_halo = n // HALO
    mod_spec = pl.BlockSpec((None, 1, D_MODEL), lambda i: (mod_row(i), 0, 0))
    tok = lambda width: pl.BlockSpec((tm, width), lambda i: (i, 0))
    tok_t = pl.BlockSpec((SSD_HEADS, tm), lambda i: (0, i))
    prev_spec = pl.BlockSpec((HALO, D_MODEL), lambda i: (jnp.maximum(i * hb - 1, 0), 0))
    next_spec = pl.BlockSpec((HALO, D_MODEL),
                             lambda i: (jnp.minimum((i + 1) * hb, n_halo - 1), 0))
    weights = [w["wz"], w["wx"], w["wdf"], w["wdb"], w["wdft"], w["wdbt"], w["wa"], w["wg"],
               w["bf"], w["bb"], w["bft"], w["bbt"], w["ssd_cw"], w["ssd_cb"]]
    return pl.pallas_call(
        functools.partial(_in_proj_kernel, tm, seqlen // tm),
        grid=(n // tm,),
        in_specs=[tok(D_MODEL), prev_spec, next_spec, mod_spec, mod_spec,
                  _const_spec((1, D_MODEL))]
                 + [_const_spec(a.shape) for a in weights],
        scratch_shapes=[pltpu.VMEM((tm + 2 * HALO, D_MODEL), BF16)]
                       + [pltpu.VMEM((tm + 2 * HALO, CONV_CH // XBC_SPLIT), F32)] * XBC_SPLIT,
        out_specs=[tok(SSD_WIDTH), tok(CONV_CH), tok(LANES), tok(LANES), tok_t, tok_t,
                   tok(CF_WIDTH)],
        out_shape=[jax.ShapeDtypeStruct((n, SSD_WIDTH), BF16),
                   jax.ShapeDtypeStruct((n, CONV_CH), BF16),
                   jax.ShapeDtypeStruct((n, LANES), F32),
                   jax.ShapeDtypeStruct((n, LANES), F32),
                   jax.ShapeDtypeStruct((SSD_HEADS, n), F32),
                   jax.ShapeDtypeStruct((SSD_HEADS, n), F32),
                   jax.ShapeDtypeStruct((n, CF_WIDTH), BF16)],
        compiler_params=_params(1),
        name="in_proj",
    )(x2d, x2d, x2d, sc, sh, g, *weights)


LOG2E = 1.4426950408889634
N_PAIRS = SSD_HEADS // 2
SSD_CHUNKS_PER_STEP = 4


def _ssd_kernel(fwd, zero_init, cps, *refs):
    refs = list(refs)
    xbc_ref, dt_ref, dtt_ref, alr_ref, alc_ref = refs[:5]
    del refs[:5]
    init_ref = None if zero_init else refs.pop(0)
    if fwd:
        yb_ref, z_ref, dskip_ref, nrm_ref = refs[:4]
        del refs[:4]
    y_ref, fin_ref = refs[:2]
    s_refs = refs[2:2 + N_PAIRS]
    yacc_refs = refs[2 + N_PAIRS:]
    j = pl.program_id(1)

    @pl.when(j == 0)
    def _():
        for pair in range(N_PAIRS):
            if zero_init:
                s_refs[pair][...] = jnp.zeros((D_STATE, LANES), F32)
            else:
                s_refs[pair][...] = init_ref[:, pair * LANES:(pair + 1) * LANES]

    ri = lax.broadcasted_iota(jnp.int32, (CHUNK, CHUNK), 0)
    ci = lax.broadcasted_iota(jnp.int32, (CHUNK, CHUNK), 1)
    keep = (ri >= ci) if fwd else (ri <= ci)
    tri = keep.astype(F32)
    lane = lax.broadcasted_iota(jnp.int32, (1, LANES), 1)
    a_row = jnp.where(lane < SSD_HEADS, -jnp.exp(alr_ref[...]) * LOG2E, 0.0)
    a_col = -jnp.exp(alc_ref[...]) * LOG2E
    half = lane < SSD_HEAD_DIM
    nt = (((1,), (1,)), ((), ()))
    heads_per_group = SSD_HEADS // N_GROUPS
    end = CHUNK - 1 if fwd else 0

    pre = []
    for sc in range(cps):
        rows = slice(sc * CHUNK, (sc + 1) * CHUNK)
        dt_row = dtt_ref[:, rows]
        cum_col = jnp.dot(tri, dt_ref[rows, :] * a_row, precision=HIGHEST,
                          preferred_element_type=F32)
        cum_row = lax.dot_general(dt_row * a_col, tri, nt, precision=HIGHEST,
                                  preferred_element_type=F32)
        cum_end = cum_row[:, end:end + 1]
        p = {
            "cum_col": cum_col,
            "wgt_row": jnp.exp2(cum_end - cum_row) * dt_row,
            "edec": jnp.exp2(cum_end),
            "ecol": jnp.exp2(cum_col),
            "src_row": cum_row - jnp.log2(dt_row),
            "cb": [], "cg": [], "bgt": [],
        }
        for grp in range(N_GROUPS):
            b_cols = slice(SSD_WIDTH + grp * D_STATE, SSD_WIDTH + (grp + 1) * D_STATE)
            c_cols = slice(SSD_WIDTH + (N_GROUPS + grp) * D_STATE,
                           SSD_WIDTH + (N_GROUPS + grp + 1) * D_STATE)
            p["cb"].append(lax.dot_general(xbc_ref[rows, c_cols], xbc_ref[rows, b_cols], nt,
                                           preferred_element_type=F32))
            p["cg"].append(xbc_ref[rows, c_cols].astype(F32))
            p["bgt"].append(xbc_ref[rows, b_cols].astype(F32).T)
        pre.append(p)

    for sc in (range(cps) if fwd else reversed(range(cps))):
        rows = slice(sc * CHUNK, (sc + 1) * CHUNK)
        p = pre[sc]
        cum_col, wgt_row, edec, ecol, src_row = (
            p["cum_col"], p["wgt_row"], p["edec"], p["ecol"], p["src_row"])
        for grp in range(N_GROUPS):
            cb, cg, bgt = p["cb"][grp], p["cg"][grp], p["bgt"][grp]
            for pair in range(grp * heads_per_group // 2, (grp + 1) * heads_per_group // 2):
                lanes = slice(pair * LANES, (pair + 1) * LANES)
                xp = xbc_ref[rows, lanes]
                zero = jnp.zeros_like(xp)
                x_bd = jnp.concatenate(
                    [jnp.where(half, xp, zero), jnp.where(half, zero, xp)], axis=0)
                sp = s_refs[pair][...]
                spb = sp.astype(BF16)
                s_bd = jnp.concatenate(
                    [jnp.where(half, spb, zero), jnp.where(half, zero, spb)], axis=0)
                m_parts, c_parts, b_parts = [], [], []
                for h in (2 * pair, 2 * pair + 1):
                    seg = jnp.where(keep, cum_col[:, h:h + 1] - src_row[h:h + 1, :], -1e30)
                    m_parts.append((cb * jnp.exp2(seg)).astype(BF16))
                    c_parts.append((cg * ecol[:, h:h + 1]).astype(BF16))
                    b_parts.append((bgt * wgt_row[h:h + 1, :]).astype(BF16))
                lhs = jnp.concatenate(m_parts + c_parts, axis=1)
                rhs = jnp.concatenate([x_bd, s_bd], axis=0)
                y_pair = _dot(lhs, rhs)
                if fwd:
                    yacc_refs[sc * N_PAIRS + pair][...] = (
                        y_pair + xp.astype(F32) * dskip_ref[:, lanes])
                else:
                    y_ref[rows, lanes] = y_pair.astype(BF16)
                dec = jnp.where(half, edec[2 * pair:2 * pair + 1, :],
                                edec[2 * pair + 1:2 * pair + 2, :])
                s_refs[pair][...] = sp * dec + _dot(jnp.concatenate(b_parts, axis=1), x_bd)

        if fwd:
            yz = []
            for pair in range(N_PAIRS):
                lanes = slice(pair * LANES, (pair + 1) * LANES)
                y = yacc_refs[sc * N_PAIRS + pair][...] + yb_ref[rows, lanes].astype(F32)
                yz.append(y * _silu(z_ref[rows, lanes].astype(F32)))
            sq = yz[0] * yz[0]
            for v in yz[1:]:
                sq = sq + v * v
            rstd = lax.rsqrt(jnp.sum(sq, axis=-1, keepdims=True) * (1.0 / SSD_WIDTH) + EPS)
            for pair in range(N_PAIRS):
                lanes = slice(pair * LANES, (pair + 1) * LANES)
                y_ref[rows, lanes] = (yz[pair] * rstd * nrm_ref[:, lanes]).astype(BF16)

    @pl.when(j == pl.num_programs(1) - 1)
    def _():
        for pair in range(N_PAIRS):
            fin_ref[:, pair * LANES:(pair + 1) * LANES] = s_refs[pair][...]


def _ssd_sweep(fwd, zero_init, bsz, seqlen, xbc, dt, dtt, alog, init, extra):
    cps = min(SSD_CHUNKS_PER_STEP, seqlen // CHUNK)
    blk = cps * CHUNK
    ns = seqlen // blk

    def block_of(j):
        return j if fwd else ns - 1 - j

    tok = lambda width: pl.BlockSpec(
        (blk, width), lambda b, j: (b * ns + block_of(j), 0))
    dtt_spec = pl.BlockSpec((SSD_HEADS, blk), lambda b, j: (0, b * ns + block_of(j)))
    state_spec = pl.BlockSpec((None, D_STATE, SSD_WIDTH), lambda b, j: (b, 0, 0))
    alog_row = jnp.pad(alog.reshape(1, SSD_HEADS), ((0, 0), (0, LANES - SSD_HEADS)))
    alog_col = alog.reshape(SSD_HEADS, 1)
    in_specs = [tok(CONV_CH), tok(LANES), dtt_spec,
                _const_spec((1, LANES)), _const_spec((SSD_HEADS, 1))]
    args = [xbc, dt, dtt, alog_row, alog_col]
    if not zero_init:
        in_specs.append(state_spec)
        args.append(init)
    if fwd:
        yb, z, dskip, nrm = extra
        in_specs += [tok(SSD_WIDTH), tok(SSD_WIDTH), _const_spec((1, SSD_WIDTH)),
                     _const_spec((1, SSD_WIDTH))]
        args += [yb, z, dskip, nrm]
    return pl.pallas_call(
        functools.partial(_ssd_kernel, fwd, zero_init, cps),
        grid=(bsz, ns),
        in_specs=in_specs,
        out_specs=[tok(SSD_WIDTH), state_spec],
        out_shape=[jax.ShapeDtypeStruct((bsz * seqlen, SSD_WIDTH), BF16),
                   jax.ShapeDtypeStruct((bsz, D_STATE, SSD_WIDTH), F32)],
        scratch_shapes=[pltpu.VMEM((D_STATE, LANES), F32)] * N_PAIRS
                       + [pltpu.VMEM((CHUNK, LANES), F32)] * (cps * N_PAIRS if fwd else 0),
        compiler_params=_params(2),
        name="ssd_fwd" if fwd else "ssd_bwd",
    )(*args)


def _cf_pitches(tm):
    seg = tm // SUBLANES
    in_pitch = -(-(seg + CF_KERNEL - 1 - 4) // 8) * 8 + 4
    out_pitch = seg + 8
    return seg, in_pitch, out_pitch


def _cf_kernel(tm, tps, u_ref, up_ref, un_ref, w_ref, b_ref, g_ref, beta_ref, o_ref,
               buf_ref, slab_ref, acc_ref):
    seg, in_pitch, out_pitch = _cf_pitches(tm)
    pad = (CF_KERNEL - 1) // 2
    t = pl.program_id(0)
    first = (t % tps) == 0
    last = (t % tps) == tps - 1
    buf_ref[0:HALO, :] = jnp.where(first, 0.0, up_ref[...].astype(F32))
    buf_ref[HALO:HALO + tm, :] = u_ref[...].astype(F32)
    buf_ref[HALO + tm:2 * HALO + tm, :] = jnp.where(last, 0.0, un_ref[...].astype(F32))
    span = seg + 2 * pad
    for jb in range(CF_WIDTH // LANES):
        for r in range(SUBLANES):
            src = HALO - pad + r * seg
            slab_ref[jb, r * in_pitch:r * in_pitch + span, :] = (
                buf_ref[src:src + span, jb * LANES:(jb + 1) * LANES])

    def lane_block(jb, carry):
        l0 = pl.multiple_of(jb * LANES, LANES)
        w = w_ref[:, pl.ds(l0, LANES)]
        taps = [jnp.broadcast_to(w[k:k + 1, :], (SUBLANES, LANES)) for k in range(CF_KERNEL)]
        bias = jnp.broadcast_to(b_ref[:, pl.ds(l0, LANES)], (SUBLANES, LANES))
        for i in range(seg):
            acc = bias
            for k in range(CF_KERNEL):
                acc = acc + taps[k] * slab_ref[jb, pl.ds(i + k, SUBLANES, stride=in_pitch), :]
            acc_ref[jb, pl.ds(i, SUBLANES, stride=out_pitch), :] = acc
        return carry

    lax.fori_loop(0, CF_WIDTH // LANES, lane_block, 0)

    nb = CF_WIDTH // LANES
    for r in range(SUBLANES):
        v = [acc_ref[jb, r * out_pitch:r * out_pitch + seg, :] for jb in range(nb)]
        mu = jnp.sum(sum(v[1:], v[0]), axis=-1, keepdims=True) * (1.0 / CF_WIDTH)
        d = [vj - mu for vj in v]
        sq = d[0] * d[0]
        for dj in d[1:]:
            sq = sq + dj * dj
        rstd = lax.rsqrt(jnp.sum(sq, axis=-1, keepdims=True) * (1.0 / CF_WIDTH) + EPS)
        for jb in range(nb):
            lanes = slice(jb * LANES, (jb + 1) * LANES)
            y = d[jb] * rstd * g_ref[:, lanes] + beta_ref[:, lanes]
            o_ref[r * seg:(r + 1) * seg, lanes] = _silu(y).astype(BF16)


def _cf_module(u, seqlen, tm, w, b, g, beta):
    n = u.shape[0]
    tps = seqlen // tm
    hb = tm // HALO
    n_halo = n // HALO
    _, in_pitch, out_pitch = _cf_pitches(tm)
    nb = CF_WIDTH // LANES
    return pl.pallas_call(
        functools.partial(_cf_kernel, tm, tps),
        grid=(n // tm,),
        in_specs=[pl.BlockSpec((tm, CF_WIDTH), lambda t: (t, 0)),
                  pl.BlockSpec((HALO, CF_WIDTH), lambda t: (jnp.maximum(t * hb - 1, 0), 0)),
                  pl.BlockSpec((HALO, CF_WIDTH),
                               lambda t: (jnp.minimum((t + 1) * hb, n_halo - 1), 0)),
                  _const_spec(w.shape), _const_spec(b.shape), _const_spec(g.shape),
                  _const_spec(beta.shape)],
        out_specs=pl.BlockSpec((tm, CF_WIDTH), lambda t: (t, 0)),
        out_shape=jax.ShapeDtypeStruct((n, CF_WIDTH), BF16),
        scratch_shapes=[pltpu.VMEM((tm + 2 * HALO, CF_WIDTH), F32),
                        pltpu.VMEM((nb, SUBLANES * in_pitch, LANES), F32),
                        pltpu.VMEM((nb, SUBLANES * out_pitch, LANES), F32)],
        compiler_params=_params(1),
        name="cf_module",
    )(u, u, u, w, b, g, beta)


MLP_KW = 256
GROUP = SUBLANES * SUBLANES


def _mlp_kernel(grid2d, nseg, tpi, *refs):
    refs = list(refs)
    y_ref, u_ref, x_ref = refs[:3]
    del refs[:3]
    if grid2d:
        yp_ref, up_ref, xp_ref, yn_ref, un_ref, xn_ref = refs[:6]
        del refs[:6]
    (g1_ref, sc_ref, sh_ref, g2_ref, wo1_ref, wo2_ref, gpost1_ref, gpre2_ref, wup_ref,
     cw_ref, cb_ref, wd_ref, gpost2_ref, o_ref) = refs[:14]
    yext_ref, uext_ref, xext_ref, hs_ref, hb_ref, x1_ref, blk_a, blk_b, act_ref, fs_ref = (
        refs[14:])
    tm = GROUP * nseg
    lo = GROUP if grid2d else 0
    ext = tm + 2 * lo
    nlb = D_MODEL // LANES
    t = pl.program_id(0)
    top = (t % tpi) == 0
    bottom = (t % tpi) == tpi - 1

    yext_ref[lo:lo + tm, :] = y_ref[...]
    uext_ref[lo:lo + tm, :] = u_ref[...]
    xext_ref[lo:lo + tm, :] = x_ref[...]
    if grid2d:
        yext_ref[0:lo, :] = yp_ref[...]
        uext_ref[0:lo, :] = up_ref[...]
        xext_ref[0:lo, :] = xp_ref[...]
        yext_ref[lo + tm:, :] = yn_ref[...]
        uext_ref[lo + tm:, :] = un_ref[...]
        xext_ref[lo + tm:, :] = xn_ref[...]
    mix = _dot(yext_ref[...], wo1_ref[...]) + _dot(uext_ref[...], wo2_ref[...])
    ms = jnp.mean(mix * mix, axis=-1, keepdims=True)
    x1 = xext_ref[...] + g1_ref[...] * (mix * lax.rsqrt(ms + EPS) * gpost1_ref[...])
    x1_ref[...] = x1[lo:lo + tm, :]
    ms2 = jnp.mean(x1 * x1, axis=-1, keepdims=True)
    h = x1 * lax.rsqrt(ms2 + EPS) * (gpre2_ref[...] * (1.0 + sc_ref[...])) + sh_ref[...]

    for j in range(nlb):
        hs_ref[j] = h[:, j * LANES:(j + 1) * LANES]
    for j in range(nlb):
        for g in range(ext // GROUP):
            grp = jnp.concatenate(
                [hs_ref[j, pl.ds(g * GROUP + i, SUBLANES, stride=SUBLANES), :]
                 for i in range(SUBLANES)], axis=0)
            if grid2d and g == 0:
                grp = jnp.where(top, 0.0, grp)
            if grid2d and g == ext // GROUP - 1:
                grp = jnp.where(bottom, 0.0, grp)
            hb_ref[g * GROUP:(g + 1) * GROUP, j * LANES:(j + 1) * LANES] = grp.astype(BF16)

    sub = lax.broadcasted_iota(jnp.int32, (SUBLANES, 1), 0)

    def shift_down(v, fill):
        edge = 0.0 if fill is None else pltpu.roll(fill, 1, axis=0)
        return jnp.where(sub > 0, pltpu.roll(v, 1, axis=0), edge)

    def shift_up(v, fill):
        edge = 0.0 if fill is None else pltpu.roll(fill, SUBLANES - 1, axis=0)
        return jnp.where(sub < SUBLANES - 1, pltpu.roll(v, SUBLANES - 1, axis=0), edge)

    def project(k0, blk):
        blk[:, 0:MLP_KW] = _dot(hb_ref[...], wup_ref[:, pl.ds(k0, MLP_KW)])
        blk[:, MLP_KW:2 * MLP_KW] = _dot(hb_ref[...], wup_ref[:, pl.ds(D_FF + k0, MLP_KW)])

    def conv_vreg(blk, s, i, cols, w, bias):
        def column(g, k, dc):
            def vreg(gg):
                r0 = gg * GROUP + k * SUBLANES
                return blk[r0:r0 + SUBLANES, cols]
            if grid2d:
                acc = w[dc:dc + 1, :] * vreg(g)
                for dr in (1, 2):
                    acc = acc + w[3 * dr + dc:3 * dr + dc + 1, :] * vreg(g + dr)
                return acc
            return w[3 + dc:4 + dc, :] * vreg(g)

        if i > 0:
            left = column(s, i - 1, 0)
        else:
            before = column(s - 1, SUBLANES - 1, 0) if (not grid2d and s > 0) else None
            left = shift_down(column(s, SUBLANES - 1, 0), before)
        if i < SUBLANES - 1:
            right = column(s, i + 1, 2)
        else:
            after = column(s + 1, 0, 2) if (not grid2d and s + 1 < nseg) else None
            right = shift_up(column(s, 0, 2), after)
        return column(s, i, 1) + bias + left + right

    def gate_slice(k0, blk):
        for q in range(MLP_KW // LANES):
            lg = pl.multiple_of(k0 + q * LANES, LANES)
            lv = pl.multiple_of(D_FF + k0 + q * LANES, LANES)
            wg, wv = cw_ref[:, pl.ds(lg, LANES)], cw_ref[:, pl.ds(lv, LANES)]
            bg, bv = cb_ref[:, pl.ds(lg, LANES)], cb_ref[:, pl.ds(lv, LANES)]
            cg = slice(q * LANES, (q + 1) * LANES)
            cv = slice(MLP_KW + q * LANES, MLP_KW + (q + 1) * LANES)
            for s in range(nseg):
                for i0 in range(0, SUBLANES, 2):
                    act = jnp.concatenate(
                        [_silu(conv_vreg(blk, s, i, cg, wg, bg))
                         * conv_vreg(blk, s, i, cv, wv, bv) for i in (i0, i0 + 1)], axis=0)
                    r0 = s * GROUP + i0 * SUBLANES
                    act_ref[r0:r0 + 2 * SUBLANES, pl.ds(lg, LANES)] = act.astype(BF16)

    def slice_pair(jj, carry):
        k0 = pl.multiple_of(jj * (2 * MLP_KW), 2 * MLP_KW)
        project(k0 + MLP_KW, blk_b)
        gate_slice(k0, blk_a)
        project(k0 + 2 * MLP_KW, blk_a)
        gate_slice(k0 + MLP_KW, blk_b)
        return carry

    n_slices = D_FF // MLP_KW
    assert n_slices % 2 == 1 and n_slices * MLP_KW == D_FF
    project(0, blk_a)
    lax.fori_loop(0, (n_slices - 1) // 2, slice_pair, 0)
    gate_slice((n_slices - 1) * MLP_KW, blk_a)

    f = _dot(act_ref[...], wd_ref[...])
    msf = jnp.mean(f * f, axis=-1, keepdims=True)
    fn = f * lax.rsqrt(msf + EPS) * gpost2_ref[...]
    for j in range(nlb):
        for g in range(nseg):
            for i in range(SUBLANES):
                r0 = g * GROUP + i * SUBLANES
                fs_ref[j, pl.ds(g * GROUP + i, SUBLANES, stride=SUBLANES), :] = (
                    fn[r0:r0 + SUBLANES, j * LANES:(j + 1) * LANES])
    for j in range(nlb):
        lanes = slice(j * LANES, (j + 1) * LANES)
        o_ref[:, lanes] = x1_ref[:, lanes] + g2_ref[:, lanes] * fs_ref[j]


def _mlp(y, u, x2d, mod_row, grid2d, nseg, tpi, g1, sc, sh, g2, w):
    n = x2d.shape[0]
    seg = GROUP
    tm = seg * nseg
    ext = tm + 2 * seg if grid2d else tm
    nlb = D_MODEL // LANES
    n_seg_total = n // seg
    mod_spec = pl.BlockSpec((None, 1, D_MODEL), lambda i: (mod_row(i), 0, 0))
    tok = pl.BlockSpec((tm, D_MODEL), lambda i: (i, 0))
    prev_spec = pl.BlockSpec((seg, D_MODEL), lambda i: (jnp.maximum(i * nseg - 1, 0), 0))
    next_spec = pl.BlockSpec(
        (seg, D_MODEL), lambda i: (jnp.minimum((i + 1) * nseg, n_seg_total - 1), 0))
    halo_specs = [prev_spec] * 3 + [next_spec] * 3 if grid2d else []
    halo_args = [y, u, x2d, y, u, x2d] if grid2d else []
    weights = [w["wo1"], w["wo2"], w["gpost1"], w["gpre2"], w["wup"], w["ffn_cw"],
               w["ffn_cb"], w["wd"], w["gpost2"]]
    return pl.pallas_call(
        functools.partial(_mlp_kernel, grid2d, nseg, tpi),
        grid=(n // tm,),
        in_specs=[tok, tok, tok] + halo_specs + [mod_spec] * 4
                 + [_const_spec(a.shape) for a in weights],
        out_specs=tok,
        out_shape=jax.ShapeDtypeStruct((n, D_MODEL), F32),
        scratch_shapes=[pltpu.VMEM((ext, D_MODEL), BF16),
                        pltpu.VMEM((ext, D_MODEL), BF16),
                        pltpu.VMEM((ext, D_MODEL), F32),
                        pltpu.VMEM((nlb, ext, LANES), F32),
                        pltpu.VMEM((ext, D_MODEL), BF16),
                        pltpu.VMEM((tm, D_MODEL), F32),
                        pltpu.VMEM((ext, 2 * MLP_KW), F32),
                        pltpu.VMEM((ext, 2 * MLP_KW), F32),
                        pltpu.VMEM((tm, D_FF), BF16),
                        pltpu.VMEM((nlb, tm, LANES), F32)],
        compiler_params=_params(1),
        name="mlp_grid" if grid2d else "mlp_seq",
    )(y, u, x2d, *halo_args, g1, sc, sh, g2, *weights)


def _state_to_kernel_layout(s):
    b = s.shape[0]
    return jnp.transpose(s.reshape(b, SSD_WIDTH, D_STATE), (0, 2, 1))


def _state_from_kernel_layout(s):
    b = s.shape[0]
    return jnp.transpose(s, (0, 2, 1)).reshape(b, SSD_HEADS, SSD_HEAD_DIM, D_STATE)


def _trunk_layer(x, mod, mod_row_of_batch, w, init_f, init_b, rows):
    bsz, seqlen, _ = x.shape
    x2d = x.reshape(bsz * seqlen, D_MODEL)
    sh1, sc1, g1, sh2, sc2, g2 = mod
    latent = rows is not None
    tm = 512 if latent else seqlen
    tps = seqlen // tm
    mod_row = lambda i: mod_row_of_batch(i // tps)

    z, xbc, dtf, dtb, dtft, dtbt, u = _in_proj(x2d, seqlen, mod_row, tm, sc1, sh1,
                                               w["gpre1"], w)

    zero_init = init_f is None
    yb, fin_b = _ssd_sweep(False, zero_init, bsz, seqlen, xbc, dtb, dtbt, w["alog_b"],
                           init_b, None)
    y, fin_f = _ssd_sweep(True, zero_init, bsz, seqlen, xbc, dtf, dtft, w["alog_f"],
                          init_f, (yb, z, w["dskip"], w["ssd_norm"]))

    ucf = _cf_module(u, seqlen, tm, w["cf_w"], w["cf_b"], w["cf_g"], w["cf_beta"])
    nseg = tm // GROUP
    if latent:
        assert GRID_W == GROUP
        out = _mlp(y, ucf, x2d, mod_row, True, nseg, rows // nseg, g1, sc2, sh2, g2, w)
    else:
        out = _mlp(y, ucf, x2d, mod_row, False, nseg, 1, g1, sc2, sh2, g2, w)
    return out.reshape(bsz, seqlen, D_MODEL), fin_f, fin_b


def _layer_weights(l, w_in, w_ssd_conv, b_ssd_conv, a_log_fwd, a_log_bwd, dt_bias_fwd,
                   dt_bias_bwd, d_skip, ssd_norm, w_cf_conv, b_cf_conv, cf_ln_g, cf_ln_b,
                   w_out, norm_mix_pre, norm_mix_post, norm_ffn_pre, norm_ffn_post,
                   w_ffn_up, w_ffn_conv, b_ffn_conv, w_ffn_down):
    wi = w_in[l].astype(BF16)
    o = 0
    wz = wi[:, o:o + SSD_WIDTH]; o += SSD_WIDTH
    wx = wi[:, o:o + CONV_CH]; o += CONV_CH
    wdf = wi[:, o:o + SSD_HEADS]; o += SSD_HEADS
    wdb = wi[:, o:o + SSD_HEADS]; o += SSD_HEADS
    wa = wi[:, o:o + CF_WIDTH]; o += CF_WIDTH
    wg = wi[:, o:o + CF_WIDTH]
    pad_l = lambda a: jnp.pad(a, ((0, 0), (0, LANES - a.shape[1])))
    row = lambda a: a.reshape(1, -1)
    wo = w_out[l].astype(BF16)
    cf_w = jnp.pad(w_cf_conv[l], ((0, 32 - CF_KERNEL), (0, 0)))
    ffn_cw = jnp.pad(w_ffn_conv[l].reshape(9, 2 * D_FF), ((0, 7), (0, 0)))
    return {
        "wz": wz, "wx": wx, "wdf": pad_l(wdf), "wdb": pad_l(wdb),
        "wdft": wdf.T, "wdbt": wdb.T, "wa": wa, "wg": wg,
        "bf": pad_l(row(dt_bias_fwd[l])), "bb": pad_l(row(dt_bias_bwd[l])),
        "bft": dt_bias_fwd[l].reshape(-1, 1), "bbt": dt_bias_bwd[l].reshape(-1, 1),
        "gpre1": row(norm_mix_pre[l]), "gpost1": row(norm_mix_post[l]),
        "gpre2": row(norm_ffn_pre[l]), "gpost2": row(norm_ffn_post[l]),
        "alog_f": a_log_fwd[l], "alog_b": a_log_bwd[l],
        "ssd_cw": jnp.pad(w_ssd_conv[l], ((0, 5), (0, 0))), "ssd_cb": row(b_ssd_conv[l]),
        "dskip": row(jnp.repeat(d_skip[l], SSD_HEAD_DIM)), "ssd_norm": row(ssd_norm[l]),
        "cf_w": cf_w, "cf_b": row(b_cf_conv[l]), "cf_g": row(cf_ln_g[l]),
        "cf_beta": row(cf_ln_b[l]),
        "wo1": wo[:SSD_WIDTH], "wo2": wo[SSD_WIDTH:],
        "wup": w_ffn_up[l].astype(BF16), "ffn_cw": ffn_cw, "ffn_cb": row(b_ffn_conv[l]),
        "wd": w_ffn_down[l].astype(BF16),
    }


def kernel(x_prompt, x_sample, state_ssd_fwd, state_ssd_bwd, c, c_ctx, w_ada, b_ada, norm_mix_pre, norm_mix_post, w_in, w_ssd_conv, b_ssd_conv, a_log_fwd, a_log_bwd, dt_bias_fwd, dt_bias_bwd, d_skip, ssd_norm, w_cf_conv, b_cf_conv, cf_ln_g, cf_ln_b, w_out, norm_ffn_pre, norm_ffn_post, w_ffn_up, w_ffn_conv, b_ffn_conv, w_ffn_down):
    depth = w_ada.shape[0]
    dec_batch = x_sample.shape[0]
    rows = x_sample.shape[1] // GRID_W
    ctx_row = dec_batch
    n_cond = -(-(dec_batch + 1) // 8) * 8
    cond = jnp.zeros((n_cond, D_MODEL), F32)
    cond = cond.at[:dec_batch].set(c).at[ctx_row].set(c_ctx)

    xp, xl = x_prompt, x_sample
    new_f, new_b = [], []
    for l in range(depth):
        w = _layer_weights(l, w_in, w_ssd_conv, b_ssd_conv, a_log_fwd, a_log_bwd,
                           dt_bias_fwd, dt_bias_bwd, d_skip, ssd_norm, w_cf_conv, b_cf_conv,
                           cf_ln_g, cf_ln_b, w_out, norm_mix_pre, norm_mix_post,
                           norm_ffn_pre, norm_ffn_post, w_ffn_up, w_ffn_conv, b_ffn_conv,
                           w_ffn_down)
        mod = _modulation(cond, w_ada[l], b_ada[l])
        mod = [m.reshape(n_cond, 1, D_MODEL) for m in jnp.split(mod, 6, axis=-1)]
        xp, s_f, s_b = _trunk_layer(xp, mod, lambda b: ctx_row, w, None, None, None)
        new_f.append(_state_from_kernel_layout(s_f))
        new_b.append(_state_from_kernel_layout(s_b))
        xl, _, _ = _trunk_layer(xl, mod, lambda b: b, w,
                                _state_to_kernel_layout(state_ssd_fwd[:, l]),
                                _state_to_kernel_layout(state_ssd_bwd[:, l]), rows)
    return (xp, xl, jnp.stack(new_f, axis=1), jnp.stack(new_b, axis=1))
```

```python
import functools

import jax
import jax.numpy as jnp
from jax import lax
from jax.experimental import pallas as pl
from jax.experimental.pallas import tpu as pltpu

D_MODEL = 1024
GRID_W = 64
SSD_WIDTH = 1024
SSD_HEAD_DIM = 64
SSD_HEADS = 16
N_GROUPS = 2
D_STATE = 128
CHUNK = 128
CONV_CH = SSD_WIDTH + 2 * N_GROUPS * D_STATE
CF_WIDTH = 1024
CF_KERNEL = 31
D_FF = 2816
EPS = 1e-6

LANES = 128
SUBLANES = 8
HALO = 16
VMEM_LIMIT = 56 * 1024 * 1024

F32 = jnp.float32
BF16 = jnp.bfloat16
HIGHEST = lax.Precision.HIGHEST


def _params(n_axes):
    return pltpu.CompilerParams(
        dimension_semantics=("arbitrary",) * n_axes, vmem_limit_bytes=VMEM_LIMIT)


def _const_spec(shape):
    nd = len(shape)
    return pl.BlockSpec(shape, lambda *_: (0,) * nd, pipeline_mode=pl.Buffered(1))


def _silu(v):
    return v * jax.nn.sigmoid(v)


def _softplus(v):
    return jnp.maximum(v, 0.0) + jnp.log1p(jnp.exp(-jnp.abs(v)))


def _dot(a, b):
    return jnp.dot(a, b, preferred_element_type=F32)


def _mod_kernel(c_ref, w_ref, b_ref, o_ref):
    s = _silu(c_ref[...])
    o_ref[...] = jnp.dot(s, w_ref[...], precision=HIGHEST,
                         preferred_element_type=F32) + b_ref[...]


def _modulation(cond, w_ada, b_ada):
    rows = cond.shape[0]
    n = w_ada.shape[1]
    tn = 1024
    return pl.pallas_call(
        _mod_kernel,
        grid=(n // tn,),
        in_specs=[_const_spec((rows, D_MODEL)),
                  pl.BlockSpec((D_MODEL, tn), lambda j: (0, j)),
                  pl.BlockSpec((1, tn), lambda j: (0, j))],
        out_specs=pl.BlockSpec((rows, tn), lambda j: (0, j)),
        out_shape=jax.ShapeDtypeStruct((rows, n), F32),
        compiler_params=_params(1),
        name="modulation",
    )(cond, w_ada, b_ada.reshape(1, n))


XBC_SPLIT = 3
IN_PROJ_TILE = 1024


def _in_proj_kernel(tm, tps, x_ref, xp_ref, xn_ref, sc_ref, sh_ref, g_ref,
                    wz_ref, wx_ref, wdf_ref, wdb_ref, wdft_ref, wdbt_ref, wa_ref, wg_ref,
                    bf_ref, bb_ref, bft_ref, bbt_ref, cw_ref, cb_ref,
                    z_ref, xbc_ref, dtf_ref, dtb_ref, dtft_ref, dtbt_ref, u_ref,
                    hb_ref, *ext_refs):
    t = pl.program_id(0)
    first = (t % tps) == 0
    last = (t % tps) == tps - 1
    scale = g_ref[...] * (1.0 + sc_ref[...])
    shift = sh_ref[...]

    def mod_norm(x):
        ms = jnp.mean(x * x, axis=-1, keepdims=True)
        return (x * lax.rsqrt(ms + EPS) * scale + shift).astype(BF16)

    zero = jnp.zeros((HALO, D_MODEL), BF16)
    hb_ref[0:HALO, :] = jnp.where(first, zero, mod_norm(xp_ref[...]))
    hb_ref[HALO:HALO + tm, :] = mod_norm(x_ref[...])
    hb_ref[HALO + tm:2 * HALO + tm, :] = jnp.where(last, zero, mod_norm(xn_ref[...]))
    hb = hb_ref[HALO:HALO + tm, :]

    piece = CONV_CH // XBC_SPLIT

    def project(p):
        ext_refs[p][...] = _dot(hb_ref[...], wx_ref[:, p * piece:(p + 1) * piece])

    def conv(p):
        cols = slice(p * piece, (p + 1) * piece)
        cw = cw_ref[:, cols]
        ext_ref = ext_refs[p]
        xc = (cw[1:2, :] * ext_ref[HALO:HALO + tm, :] + cb_ref[:, cols]
              + cw[0:1, :] * ext_ref[HALO - 1:HALO - 1 + tm, :]
              + cw[2:3, :] * ext_ref[HALO + 1:HALO + 1 + tm, :])
        xbc_ref[:, cols] = _silu(xc).astype(BF16)

    project(0)
    for p in range(1, XBC_SPLIT):
        project(p)
        conv(p - 1)
    z_ref[...] = _dot(hb, wz_ref[...]).astype(BF16)
    conv(XBC_SPLIT - 1)

    a = _dot(hb, wa_ref[...])
    g = _dot(hb, wg_ref[...])
    u_ref[...] = (a * jax.nn.sigmoid(g)).astype(BF16)
    dtf_ref[...] = _softplus(_dot(hb, wdf_ref[...]) + bf_ref[...])
    dtb_ref[...] = _softplus(_dot(hb, wdb_ref[...]) + bb_ref[...])
    nt = (((1,), (1,)), ((), ()))
    dtft_ref[...] = _softplus(
        lax.dot_general(wdft_ref[...], hb, nt, preferred_element_type=F32) + bft_ref[...])
    dtbt_ref[...] = _softplus(
        lax.dot_general(wdbt_ref[...], hb, nt, preferred_element_type=F32) + bbt_ref[...])


def _in_proj(x2d, seqlen, mod_row, tm, sc, sh, g, w):
    n = x2d.shape[0]
    hb = tm // HALO
    n_halo = n // HALO
    mod_spec = pl.BlockSpec((None, 1, D_MODEL), lambda i: (mod_row(i), 0, 0))
    tok = lambda width: pl.BlockSpec((tm, width), lambda i: (i, 0))
    tok_t = pl.BlockSpec((SSD_HEADS, tm), lambda i: (0, i))
    prev_spec = pl.BlockSpec((HALO, D_MODEL), lambda i: (jnp.maximum(i * hb - 1, 0), 0))
    next_spec = pl.BlockSpec((HALO, D_MODEL),
                             lambda i: (jnp.minimum((i + 1) * hb, n_halo - 1), 0))
    weights = [w["wz"], w["wx"], w["wdf"], w["wdb"], w["wdft"], w["wdbt"], w["wa"], w["wg"],
               w["bf"], w["bb"], w["bft"], w["bbt"], w["ssd_cw"], w["ssd_cb"]]
    return pl.pallas_call(
        functools.partial(_in_proj_kernel, tm, seqlen // tm),
        grid=(n // tm,),
        in_specs=[tok(D_MODEL), prev_spec, next_spec, mod_spec, mod_spec,
                  _const_spec((1, D_MODEL))]
                 + [_const_spec(a.shape) for a in weights],
        scratch_shapes=[pltpu.VMEM((tm + 2 * HALO, D_MODEL), BF16)]
                       + [pltpu.VMEM((tm + 2 * HALO, CONV_CH // XBC_SPLIT), F32)] * XBC_SPLIT,
        out_specs=[tok(SSD_WIDTH), tok(CONV_CH), tok(LANES), tok(LANES), tok_t, tok_t,
                   tok(CF_WIDTH)],
        out_shape=[jax.ShapeDtypeStruct((n, SSD_WIDTH), BF16),
                   jax.ShapeDtypeStruct((n, CONV_CH), BF16),
                   jax.ShapeDtypeStruct((n, LANES), F32),
                   jax.ShapeDtypeStruct((n, LANES), F32),
                   jax.ShapeDtypeStruct((SSD_HEADS, n), F32),
                   jax.ShapeDtypeStruct((SSD_HEADS, n), F32),
                   jax.ShapeDtypeStruct((n, CF_WIDTH), BF16)],
        compiler_params=_params(1),
        name="in_proj",
    )(x2d, x2d, x2d, sc, sh, g, *weights)


LOG2E = 1.4426950408889634
N_PAIRS = SSD_HEADS // 2
SSD_CHUNKS_PER_STEP = 4


def _ssd_kernel(fwd, zero_init, cps, *refs):
    refs = list(refs)
    xbc_ref, dt_ref, dtt_ref, alr_ref, alc_ref = refs[:5]
    del refs[:5]
    init_ref = None if zero_init else refs.pop(0)
    if fwd:
        yb_ref, z_ref, dskip_ref, nrm_ref = refs[:4]
        del refs[:4]
    y_ref, fin_ref = refs[:2]
    s_refs = refs[2:2 + N_PAIRS]
    yacc_refs = refs[2 + N_PAIRS:]
    j = pl.program_id(1)

    @pl.when(j == 0)
    def _():
        for pair in range(N_PAIRS):
            if zero_init:
                s_refs[pair][...] = jnp.zeros((D_STATE, LANES), F32)
            else:
                s_refs[pair][...] = init_ref[pair * LANES:(pair + 1) * LANES, :].T

    ri = lax.broadcasted_iota(jnp.int32, (CHUNK, CHUNK), 0)
    ci = lax.broadcasted_iota(jnp.int32, (CHUNK, CHUNK), 1)
    keep = (ri >= ci) if fwd else (ri <= ci)
    tri = keep.astype(F32)
    lane = lax.broadcasted_iota(jnp.int32, (1, LANES), 1)
    a_row = jnp.where(lane < SSD_HEADS, -jnp.exp(alr_ref[...]) * LOG2E, 0.0)
    a_col = -jnp.exp(alc_ref[...]) * LOG2E
    half = lane < SSD_HEAD_DIM
    nt = (((1,), (1,)), ((), ()))
    heads_per_group = SSD_HEADS // N_GROUPS
    end = CHUNK - 1 if fwd else 0

    pre = []
    for sc in range(cps):
        rows = slice(sc * CHUNK, (sc + 1) * CHUNK)
        dt_row = dtt_ref[:, rows]
        cum_col = jnp.dot(tri, dt_ref[rows, :] * a_row, precision=HIGHEST,
                          preferred_element_type=F32)
        cum_row = lax.dot_general(dt_row * a_col, tri, nt, precision=HIGHEST,
                                  preferred_element_type=F32)
        cum_end = cum_row[:, end:end + 1]
        p = {
            "cum_col": cum_col,
            "wgt_row": jnp.exp2(cum_end - cum_row) * dt_row,
            "edec": jnp.exp2(cum_end),
            "ecol": jnp.exp2(cum_col),
            "src_row": cum_row - jnp.log2(dt_row),
            "cb": [], "cg": [], "bgt": [],
        }
        for grp in range(N_GROUPS):
            b_cols = slice(SSD_WIDTH + grp * D_STATE, SSD_WIDTH + (grp + 1) * D_STATE)
            c_cols = slice(SSD_WIDTH + (N_GROUPS + grp) * D_STATE,
                           SSD_WIDTH + (N_GROUPS + grp + 1) * D_STATE)
            p["cb"].append(lax.dot_general(xbc_ref[rows, c_cols], xbc_ref[rows, b_cols], nt,
                                           preferred_element_type=F32))
            p["cg"].append(xbc_ref[rows, c_cols].astype(F32))
            p["bgt"].append(xbc_ref[rows, b_cols].astype(F32).T)
        pre.append(p)

    for sc in (range(cps) if fwd else reversed(range(cps))):
        rows = slice(sc * CHUNK, (sc + 1) * CHUNK)
        p = pre[sc]
        cum_col, wgt_row, edec, ecol, src_row = (
            p["cum_col"], p["wgt_row"], p["edec"], p["ecol"], p["src_row"])
        for grp in range(N_GROUPS):
            cb, cg, bgt = p["cb"][grp], p["cg"][grp], p["bgt"][grp]
            for pair in range(grp * heads_per_group // 2, (grp + 1) * heads_per_group // 2):
                lanes = slice(pair * LANES, (pair + 1) * LANES)
                xp = xbc_ref[rows, lanes]
                zero = jnp.zeros_like(xp)
                x_bd = jnp.concatenate(
                    [jnp.where(half, xp, zero), jnp.where(half, zero, xp)], axis=0)
                sp = s_refs[pair][...]
                spb = sp.astype(BF16)
                s_bd = jnp.concatenate(
                    [jnp.where(half, spb, zero), jnp.where(half, zero, spb)], axis=0)
                m_parts, c_parts, b_parts = [], [], []
                for h in (2 * pair, 2 * pair + 1):
                    seg = jnp.where(keep, cum_col[:, h:h + 1] - src_row[h:h + 1, :], -1e30)
                    m_parts.append((cb * jnp.exp2(seg)).astype(BF16))
                    c_parts.append((cg * ecol[:, h:h + 1]).astype(BF16))
                    b_parts.append((bgt * wgt_row[h:h + 1, :]).astype(BF16))
                lhs = jnp.concatenate(m_parts + c_parts, axis=1)
                rhs = jnp.concatenate([x_bd, s_bd], axis=0)
                y_pair = _dot(lhs, rhs)
                if fwd:
                    yacc_refs[sc * N_PAIRS + pair][...] = (
                        y_pair + xp.astype(F32) * dskip_ref[:, lanes])
                else:
                    y_ref[rows, lanes] = y_pair.astype(BF16)
                dec = jnp.where(half, edec[2 * pair:2 * pair + 1, :],
                                edec[2 * pair + 1:2 * pair + 2, :])
                s_refs[pair][...] = sp * dec + _dot(jnp.concatenate(b_parts, axis=1), x_bd)

        if fwd:
            yz = []
            for pair in range(N_PAIRS):
                lanes = slice(pair * LANES, (pair + 1) * LANES)
                y = yacc_refs[sc * N_PAIRS + pair][...] + yb_ref[rows, lanes].astype(F32)
                yz.append(y * _silu(z_ref[rows, lanes].astype(F32)))
            sq = yz[0] * yz[0]
            for v in yz[1:]:
                sq = sq + v * v
            rstd = lax.rsqrt(jnp.sum(sq, axis=-1, keepdims=True) * (1.0 / SSD_WIDTH) + EPS)
            for pair in range(N_PAIRS):
                lanes = slice(pair * LANES, (pair + 1) * LANES)
                y_ref[rows, lanes] = (yz[pair] * rstd * nrm_ref[:, lanes]).astype(BF16)

    @pl.when(j == pl.num_programs(1) - 1)
    def _():
        for pair in range(N_PAIRS):
            fin_ref[pair * LANES:(pair + 1) * LANES, :] = s_refs[pair][...].T


def _ssd_sweep(fwd, zero_init, bsz, seqlen, xbc, dt, dtt, alog, init, extra):
    cps = min(SSD_CHUNKS_PER_STEP, seqlen // CHUNK)
    blk = cps * CHUNK
    ns = seqlen // blk

    def block_of(j):
        return j if fwd else ns - 1 - j

    tok = lambda width: pl.BlockSpec(
        (blk, width), lambda b, j: (b * ns + block_of(j), 0))
    dtt_spec = pl.BlockSpec((SSD_HEADS, blk), lambda b, j: (0, b * ns + block_of(j)))
    state_spec = pl.BlockSpec((None, SSD_WIDTH, D_STATE), lambda b, j: (b, 0, 0))
    alog_row = jnp.pad(alog.reshape(1, SSD_HEADS), ((0, 0), (0, LANES - SSD_HEADS)))
    alog_col = alog.reshape(SSD_HEADS, 1)
    in_specs = [tok(CONV_CH), tok(LANES), dtt_spec,
                _const_spec((1, LANES)), _const_spec((SSD_HEADS, 1))]
    args = [xbc, dt, dtt, alog_row, alog_col]
    if not zero_init:
        in_specs.append(state_spec)
        args.append(init)
    if fwd:
        yb, z, dskip, nrm = extra
        in_specs += [tok(SSD_WIDTH), tok(SSD_WIDTH), _const_spec((1, SSD_WIDTH)),
                     _const_spec((1, SSD_WIDTH))]
        args += [yb, z, dskip, nrm]
    return pl.pallas_call(
        functools.partial(_ssd_kernel, fwd, zero_init, cps),
        grid=(bsz, ns),
        in_specs=in_specs,
        out_specs=[tok(SSD_WIDTH), state_spec],
        out_shape=[jax.ShapeDtypeStruct((bsz * seqlen, SSD_WIDTH), BF16),
                   jax.ShapeDtypeStruct((bsz, SSD_WIDTH, D_STATE), F32)],
        scratch_shapes=[pltpu.VMEM((D_STATE, LANES), F32)] * N_PAIRS
                       + [pltpu.VMEM((CHUNK, LANES), F32)] * (cps * N_PAIRS if fwd else 0),
        compiler_params=_params(2),
        name="ssd_fwd" if fwd else "ssd_bwd",
    )(*args)


def _cf_pitches(tm):
    seg = tm // SUBLANES
    in_pitch = -(-(seg + CF_KERNEL - 1 - 4) // 8) * 8 + 4
    out_pitch = seg + 8
    return seg, in_pitch, out_pitch


def _cf_kernel(tm, tps, u_ref, up_ref, un_ref, w_ref, b_ref, g_ref, beta_ref, o_ref,
               buf_ref, slab_ref, acc_ref):
    seg, in_pitch, out_pitch = _cf_pitches(tm)
    pad = (CF_KERNEL - 1) // 2
    t = pl.program_id(0)
    first = (t % tps) == 0
    last = (t % tps) == tps - 1
    buf_ref[0:HALO, :] = jnp.where(first, 0.0, up_ref[...].astype(F32))
    buf_ref[HALO:HALO + tm, :] = u_ref[...].astype(F32)
    buf_ref[HALO + tm:2 * HALO + tm, :] = jnp.where(last, 0.0, un_ref[...].astype(F32))
    span = seg + 2 * pad
    for jb in range(CF_WIDTH // LANES):
        for r in range(SUBLANES):
            src = HALO - pad + r * seg
            slab_ref[jb, r * in_pitch:r * in_pitch + span, :] = (
                buf_ref[src:src + span, jb * LANES:(jb + 1) * LANES])

    def lane_block(jb, carry):
        l0 = pl.multiple_of(jb * LANES, LANES)
        w = w_ref[:, pl.ds(l0, LANES)]
        taps = [jnp.broadcast_to(w[k:k + 1, :], (SUBLANES, LANES)) for k in range(CF_KERNEL)]
        bias = jnp.broadcast_to(b_ref[:, pl.ds(l0, LANES)], (SUBLANES, LANES))
        for i in range(seg):
            acc = bias
            for k in range(CF_KERNEL):
                acc = acc + taps[k] * slab_ref[jb, pl.ds(i + k, SUBLANES, stride=in_pitch), :]
            acc_ref[jb, pl.ds(i, SUBLANES, stride=out_pitch), :] = acc
        return carry

    lax.fori_loop(0, CF_WIDTH // LANES, lane_block, 0)

    nb = CF_WIDTH // LANES
    for r in range(SUBLANES):
        v = [acc_ref[jb, r * out_pitch:r * out_pitch + seg, :] for jb in range(nb)]
        mu = jnp.sum(sum(v[1:], v[0]), axis=-1, keepdims=True) * (1.0 / CF_WIDTH)
        d = [vj - mu for vj in v]
        sq = d[0] * d[0]
        for dj in d[1:]:
            sq = sq + dj * dj
        rstd = lax.rsqrt(jnp.sum(sq, axis=-1, keepdims=True) * (1.0 / CF_WIDTH) + EPS)
        for jb in range(nb):
            lanes = slice(jb * LANES, (jb + 1) * LANES)
            y = d[jb] * rstd * g_ref[:, lanes] + beta_ref[:, lanes]
            o_ref[r * seg:(r + 1) * seg, lanes] = _silu(y).astype(BF16)


def _cf_module(u, seqlen, tm, w, b, g, beta):
    n = u.shape[0]
    tps = seqlen // tm
    hb = tm // HALO
    n_halo = n // HALO
    _, in_pitch, out_pitch = _cf_pitches(tm)
    nb = CF_WIDTH // LANES
    return pl.pallas_call(
        functools.partial(_cf_kernel, tm, tps),
        grid=(n // tm,),
        in_specs=[pl.BlockSpec((tm, CF_WIDTH), lambda t: (t, 0)),
                  pl.BlockSpec((HALO, CF_WIDTH), lambda t: (jnp.maximum(t * hb - 1, 0), 0)),
                  pl.BlockSpec((HALO, CF_WIDTH),
                               lambda t: (jnp.minimum((t + 1) * hb, n_halo - 1), 0)),
                  _const_spec(w.shape), _const_spec(b.shape), _const_spec(g.shape),
                  _const_spec(beta.shape)],
        out_specs=pl.BlockSpec((tm, CF_WIDTH), lambda t: (t, 0)),
        out_shape=jax.ShapeDtypeStruct((n, CF_WIDTH), BF16),
        scratch_shapes=[pltpu.VMEM((tm + 2 * HALO, CF_WIDTH), F32),
                        pltpu.VMEM((nb, SUBLANES * in_pitch, LANES), F32),
                        pltpu.VMEM((nb, SUBLANES * out_pitch, LANES), F32)],
        compiler_params=_params(1),
        name="cf_module",
    )(u, u, u, w, b, g, beta)


MLP_KW = 256
GROUP = SUBLANES * SUBLANES


def _mlp_kernel(grid2d, nseg, tpi, *refs):
    refs = list(refs)
    y_ref, u_ref, x_ref = refs[:3]
    del refs[:3]
    if grid2d:
        yp_ref, up_ref, xp_ref, yn_ref, un_ref, xn_ref = refs[:6]
        del refs[:6]
    (g1_ref, sc_ref, sh_ref, g2_ref, wo1_ref, wo2_ref, gpost1_ref, gpre2_ref, wup_ref,
     cw_ref, cb_ref, wd_ref, gpost2_ref, o_ref) = refs[:14]
    yext_ref, uext_ref, xext_ref, hs_ref, hb_ref, x1_ref, blk_a, blk_b, act_ref, fs_ref = (
        refs[14:])
    tm = GROUP * nseg
    lo = GROUP if grid2d else 0
    ext = tm + 2 * lo
    nlb = D_MODEL // LANES
    t = pl.program_id(0)
    top = (t % tpi) == 0
    bottom = (t % tpi) == tpi - 1

    yext_ref[lo:lo + tm, :] = y_ref[...]
    uext_ref[lo:lo + tm, :] = u_ref[...]
    xext_ref[lo:lo + tm, :] = x_ref[...]
    if grid2d:
        yext_ref[0:lo, :] = yp_ref[...]
        uext_ref[0:lo, :] = up_ref[...]
        xext_ref[0:lo, :] = xp_ref[...]
        yext_ref[lo + tm:, :] = yn_ref[...]
        uext_ref[lo + tm:, :] = un_ref[...]
        xext_ref[lo + tm:, :] = xn_ref[...]
    mix = _dot(yext_ref[...], wo1_ref[...]) + _dot(uext_ref[...], wo2_ref[...])
    ms = jnp.mean(mix * mix, axis=-1, keepdims=True)
    x1 = xext_ref[...] + g1_ref[...] * (mix * lax.rsqrt(ms + EPS) * gpost1_ref[...])
    x1_ref[...] = x1[lo:lo + tm, :]
    ms2 = jnp.mean(x1 * x1, axis=-1, keepdims=True)
    h = x1 * lax.rsqrt(ms2 + EPS) * (gpre2_ref[...] * (1.0 + sc_ref[...])) + sh_ref[...]

    for j in range(nlb):
        hs_ref[j] = h[:, j * LANES:(j + 1) * LANES]
    for j in range(nlb):
        for g in range(ext // GROUP):
            grp = jnp.concatenate(
                [hs_ref[j, pl.ds(g * GROUP + i, SUBLANES, stride=SUBLANES), :]
                 for i in range(SUBLANES)], axis=0)
            if grid2d and g == 0:
                grp = jnp.where(top, 0.0, grp)
            if grid2d and g == ext // GROUP - 1:
                grp = jnp.where(bottom, 0.0, grp)
            hb_ref[g * GROUP:(g + 1) * GROUP, j * LANES:(j + 1) * LANES] = grp.astype(BF16)

    sub = lax.broadcasted_iota(jnp.int32, (SUBLANES, 1), 0)

    def shift_down(v, fill):
        edge = 0.0 if fill is None else pltpu.roll(fill, 1, axis=0)
        return jnp.where(sub > 0, pltpu.roll(v, 1, axis=0), edge)

    def shift_up(v, fill):
        edge = 0.0 if fill is None else pltpu.roll(fill, SUBLANES - 1, axis=0)
        return jnp.where(sub < SUBLANES - 1, pltpu.roll(v, SUBLANES - 1, axis=0), edge)

    def project(k0, blk):
        blk[:, 0:MLP_KW] = _dot(hb_ref[...], wup_ref[:, pl.ds(k0, MLP_KW)])
        blk[:, MLP_KW:2 * MLP_KW] = _dot(hb_ref[...], wup_ref[:, pl.ds(D_FF + k0, MLP_KW)])

    def conv_vreg(blk, s, i, cols, w, bias, loaded):
        def column(g, k, dc):
            def vreg(gg):
                if (gg, k) not in loaded:
                    r0 = gg * GROUP + k * SUBLANES
                    loaded[(gg, k)] = blk[r0:r0 + SUBLANES, cols]
                return loaded[(gg, k)]
            if grid2d:
                acc = w[dc:dc + 1, :] * vreg(g)
                for dr in (1, 2):
                    acc = acc + w[3 * dr + dc:3 * dr + dc + 1, :] * vreg(g + dr)
                return acc
            return w[3 + dc:4 + dc, :] * vreg(g)

        if i > 0:
            left = column(s, i - 1, 0)
        else:
            before = column(s - 1, SUBLANES - 1, 0) if (not grid2d and s > 0) else None
            left = shift_down(column(s, SUBLANES - 1, 0), before)
        if i < SUBLANES - 1:
            right = column(s, i + 1, 2)
        else:
            after = column(s + 1, 0, 2) if (not grid2d and s + 1 < nseg) else None
            right = shift_up(column(s, 0, 2), after)
        return column(s, i, 1) + bias + left + right

    def gate_slice(k0, blk):
        for q in range(MLP_KW // LANES):
            lg = pl.multiple_of(k0 + q * LANES, LANES)
            lv = pl.multiple_of(D_FF + k0 + q * LANES, LANES)
            wg, wv = cw_ref[:, pl.ds(lg, LANES)], cw_ref[:, pl.ds(lv, LANES)]
            bg, bv = cb_ref[:, pl.ds(lg, LANES)], cb_ref[:, pl.ds(lv, LANES)]
            cg = slice(q * LANES, (q + 1) * LANES)
            cv = slice(MLP_KW + q * LANES, MLP_KW + (q + 1) * LANES)
            gate_in, val_in = {}, {}
            for s in range(nseg):
                for i0 in range(0, SUBLANES, 2):
                    act = jnp.concatenate(
                        [_silu(conv_vreg(blk, s, i, cg, wg, bg, gate_in))
                         * conv_vreg(blk, s, i, cv, wv, bv, val_in) for i in (i0, i0 + 1)],
                        axis=0)
                    r0 = s * GROUP + i0 * SUBLANES
                    act_ref[r0:r0 + 2 * SUBLANES, pl.ds(lg, LANES)] = act.astype(BF16)

    def slice_pair(jj, carry):
        k0 = pl.multiple_of(jj * (2 * MLP_KW), 2 * MLP_KW)
        project(k0 + MLP_KW, blk_b)
        gate_slice(k0, blk_a)
        project(k0 + 2 * MLP_KW, blk_a)
        gate_slice(k0 + MLP_KW, blk_b)
        return carry

    n_slices = D_FF // MLP_KW
    assert n_slices % 2 == 1 and n_slices * MLP_KW == D_FF
    project(0, blk_a)
    lax.fori_loop(0, (n_slices - 1) // 2, slice_pair, 0)
    gate_slice((n_slices - 1) * MLP_KW, blk_a)

    f = _dot(act_ref[...], wd_ref[...])
    msf = jnp.mean(f * f, axis=-1, keepdims=True)
    fn = f * lax.rsqrt(msf + EPS) * gpost2_ref[...]
    for j in range(nlb):
        for g in range(nseg):
            for i in range(SUBLANES):
                r0 = g * GROUP + i * SUBLANES
                fs_ref[j, pl.ds(g * GROUP + i, SUBLANES, stride=SUBLANES), :] = (
                    fn[r0:r0 + SUBLANES, j * LANES:(j + 1) * LANES])
    for j in range(nlb):
        lanes = slice(j * LANES, (j + 1) * LANES)
        o_ref[:, lanes] = x1_ref[:, lanes] + g2_ref[:, lanes] * fs_ref[j]


def _mlp(y, u, x2d, mod_row, grid2d, nseg, tpi, g1, sc, sh, g2, w):
    n = x2d.shape[0]
    seg = GROUP
    tm = seg * nseg
    ext = tm + 2 * seg if grid2d else tm
    nlb = D_MODEL // LANES
    n_seg_total = n // seg
    mod_spec = pl.BlockSpec((None, 1, D_MODEL), lambda i: (mod_row(i), 0, 0))
    tok = pl.BlockSpec((tm, D_MODEL), lambda i: (i, 0))
    prev_spec = pl.BlockSpec((seg, D_MODEL), lambda i: (jnp.maximum(i * nseg - 1, 0), 0))
    next_spec = pl.BlockSpec(
        (seg, D_MODEL), lambda i: (jnp.minimum((i + 1) * nseg, n_seg_total - 1), 0))
    halo_specs = [prev_spec] * 3 + [next_spec] * 3 if grid2d else []
    halo_args = [y, u, x2d, y, u, x2d] if grid2d else []
    weights = [w["wo1"], w["wo2"], w["gpost1"], w["gpre2"], w["wup"], w["ffn_cw"],
               w["ffn_cb"], w["wd"], w["gpost2"]]
    return pl.pallas_call(
        functools.partial(_mlp_kernel, grid2d, nseg, tpi),
        grid=(n // tm,),
        in_specs=[tok, tok, tok] + halo_specs + [mod_spec] * 4
                 + [_const_spec(a.shape) for a in weights],
        out_specs=tok,
        out_shape=jax.ShapeDtypeStruct((n, D_MODEL), F32),
        scratch_shapes=[pltpu.VMEM((ext, D_MODEL), BF16),
                        pltpu.VMEM((ext, D_MODEL), BF16),
                        pltpu.VMEM((ext, D_MODEL), F32),
                        pltpu.VMEM((nlb, ext, LANES), F32),
                        pltpu.VMEM((ext, D_MODEL), BF16),
                        pltpu.VMEM((tm, D_MODEL), F32),
                        pltpu.VMEM((ext, 2 * MLP_KW), F32),
                        pltpu.VMEM((ext, 2 * MLP_KW), F32),
                        pltpu.VMEM((tm, D_FF), BF16),
                        pltpu.VMEM((nlb, tm, LANES), F32)],
        compiler_params=_params(1),
        name="mlp_grid" if grid2d else "mlp_seq",
    )(y, u, x2d, *halo_args, g1, sc, sh, g2, *weights)


def _state_to_kernel_layout(s):
    return s.reshape(s.shape[0], SSD_WIDTH, D_STATE)


def _state_from_kernel_layout(s):
    return s.reshape(s.shape[0], SSD_HEADS, SSD_HEAD_DIM, D_STATE)


def _trunk_layer(x, mod, mod_row_of_batch, w, init_f, init_b, rows):
    bsz, seqlen, _ = x.shape
    x2d = x.reshape(bsz * seqlen, D_MODEL)
    sh1, sc1, g1, sh2, sc2, g2 = mod
    latent = rows is not None
    tm = 512 if latent else seqlen
    tps = seqlen // tm
    mod_row = lambda i: mod_row_of_batch(i // tps)

    tm_in = min(IN_PROJ_TILE, seqlen)
    z, xbc, dtf, dtb, dtft, dtbt, u = _in_proj(
        x2d, seqlen, lambda i: mod_row_of_batch(i // (seqlen // tm_in)), tm_in, sc1, sh1,
        w["gpre1"], w)

    zero_init = init_f is None
    yb, fin_b = _ssd_sweep(False, zero_init, bsz, seqlen, xbc, dtb, dtbt, w["alog_b"],
                           init_b, None)
    y, fin_f = _ssd_sweep(True, zero_init, bsz, seqlen, xbc, dtf, dtft, w["alog_f"],
                          init_f, (yb, z, w["dskip"], w["ssd_norm"]))

    ucf = _cf_module(u, seqlen, tm, w["cf_w"], w["cf_b"], w["cf_g"], w["cf_beta"])
    nseg = tm // GROUP
    if latent:
        assert GRID_W == GROUP
        out = _mlp(y, ucf, x2d, mod_row, True, nseg, rows // nseg, g1, sc2, sh2, g2, w)
    else:
        out = _mlp(y, ucf, x2d, mod_row, False, nseg, 1, g1, sc2, sh2, g2, w)
    return out.reshape(bsz, seqlen, D_MODEL), fin_f, fin_b


def _layer_weights(l, w_in, w_ssd_conv, b_ssd_conv, a_log_fwd, a_log_bwd, dt_bias_fwd,
                   dt_bias_bwd, d_skip, ssd_norm, w_cf_conv, b_cf_conv, cf_ln_g, cf_ln_b,
                   w_out, norm_mix_pre, norm_mix_post, norm_ffn_pre, norm_ffn_post,
                   w_ffn_up, w_ffn_conv, b_ffn_conv, w_ffn_down):
    wi = w_in[l].astype(BF16)
    o = 0
    wz = wi[:, o:o + SSD_WIDTH]; o += SSD_WIDTH
    wx = wi[:, o:o + CONV_CH]; o += CONV_CH
    wdf = wi[:, o:o + SSD_HEADS]; o += SSD_HEADS
    wdb = wi[:, o:o + SSD_HEADS]; o += SSD_HEADS
    wa = wi[:, o:o + CF_WIDTH]; o += CF_WIDTH
    wg = wi[:, o:o + CF_WIDTH]
    pad_l = lambda a: jnp.pad(a, ((0, 0), (0, LANES - a.shape[1])))
    row = lambda a: a.reshape(1, -1)
    wo = w_out[l].astype(BF16)
    cf_w = jnp.pad(w_cf_conv[l], ((0, 32 - CF_KERNEL), (0, 0)))
    ffn_cw = jnp.pad(w_ffn_conv[l].reshape(9, 2 * D_FF), ((0, 7), (0, 0)))
    return {
        "wz": wz, "wx": wx, "wdf": pad_l(wdf), "wdb": pad_l(wdb),
        "wdft": wdf.T, "wdbt": wdb.T, "wa": wa, "wg": wg,
        "bf": pad_l(row(dt_bias_fwd[l])), "bb": pad_l(row(dt_bias_bwd[l])),
        "bft": dt_bias_fwd[l].reshape(-1, 1), "bbt": dt_bias_bwd[l].reshape(-1, 1),
        "gpre1": row(norm_mix_pre[l]), "gpost1": row(norm_mix_post[l]),
        "gpre2": row(norm_ffn_pre[l]), "gpost2": row(norm_ffn_post[l]),
        "alog_f": a_log_fwd[l], "alog_b": a_log_bwd[l],
        "ssd_cw": jnp.pad(w_ssd_conv[l], ((0, 5), (0, 0))), "ssd_cb": row(b_ssd_conv[l]),
        "dskip": row(jnp.repeat(d_skip[l], SSD_HEAD_DIM)), "ssd_norm": row(ssd_norm[l]),
        "cf_w": cf_w, "cf_b": row(b_cf_conv[l]), "cf_g": row(cf_ln_g[l]),
        "cf_beta": row(cf_ln_b[l]),
        "wo1": wo[:SSD_WIDTH], "wo2": wo[SSD_WIDTH:],
        "wup": w_ffn_up[l].astype(BF16), "ffn_cw": ffn_cw, "ffn_cb": row(b_ffn_conv[l]),
        "wd": w_ffn_down[l].astype(BF16),
    }


def kernel(x_prompt, x_sample, state_ssd_fwd, state_ssd_bwd, c, c_ctx, w_ada, b_ada, norm_mix_pre, norm_mix_post, w_in, w_ssd_conv, b_ssd_conv, a_log_fwd, a_log_bwd, dt_bias_fwd, dt_bias_bwd, d_skip, ssd_norm, w_cf_conv, b_cf_conv, cf_ln_g, cf_ln_b, w_out, norm_ffn_pre, norm_ffn_post, w_ffn_up, w_ffn_conv, b_ffn_conv, w_ffn_down):
    depth = w_ada.shape[0]
    dec_batch = x_sample.shape[0]
    rows = x_sample.shape[1] // GRID_W
    ctx_row = dec_batch
    n_cond = -(-(dec_batch + 1) // 8) * 8
    cond = jnp.zeros((n_cond, D_MODEL), F32)
    cond = cond.at[:dec_batch].set(c).at[ctx_row].set(c_ctx)

    xp, xl = x_prompt, x_sample
    new_f, new_b = [], []
    for l in range(depth):
        w = _layer_weights(l, w_in, w_ssd_conv, b_ssd_conv, a_log_fwd, a_log_bwd,
                           dt_bias_fwd, dt_bias_bwd, d_skip, ssd_norm, w_cf_conv, b_cf_conv,
                           cf_ln_g, cf_ln_b, w_out, norm_mix_pre, norm_mix_post,
                           norm_ffn_pre, norm_ffn_post, w_ffn_up, w_ffn_conv, b_ffn_conv,
                           w_ffn_down)
        mod = _modulation(cond, w_ada[l], b_ada[l])
        mod = [m.reshape(n_cond, 1, D_MODEL) for m in jnp.split(mod, 6, axis=-1)]
        xp, s_f, s_b = _trunk_layer(xp, mod, lambda b: ctx_row, w, None, None, None)
        new_f.append(_state_from_kernel_layout(s_f))
        new_b.append(_state_from_kernel_layout(s_b))
        xl, _, _ = _trunk_layer(xl, mod, lambda b: b, w,
                                _state_to_kernel_layout(state_ssd_fwd[:, l]),
                                _state_to_kernel_layout(state_ssd_bwd[:, l]), rows)
    return (xp, xl, jnp.stack(new_f, axis=1), jnp.stack(new_b, axis=1))
```

```python
import functools

import jax
import jax.numpy as jnp
from jax import lax
from jax.experimental import pallas as pl
from jax.experimental.pallas import tpu as pltpu

D_MODEL = 1024
GRID_W = 64
SSD_WIDTH = 1024
SSD_HEAD_DIM = 64
SSD_HEADS = 16
N_GROUPS = 2
D_STATE = 128
CHUNK = 128
CONV_CH = SSD_WIDTH + 2 * N_GROUPS * D_STATE
CF_WIDTH = 1024
CF_KERNEL = 31
D_FF = 2816
EPS = 1e-6

LANES = 128
SUBLANES = 8
HALO = 16
VMEM_LIMIT = 56 * 1024 * 1024

F32 = jnp.float32
BF16 = jnp.bfloat16
HIGHEST = lax.Precision.HIGHEST


def _params(n_axes):
    return pltpu.CompilerParams(
        dimension_semantics=("arbitrary",) * n_axes, vmem_limit_bytes=VMEM_LIMIT)


def _const_spec(shape):
    nd = len(shape)
    return pl.BlockSpec(shape, lambda *_: (0,) * nd, pipeline_mode=pl.Buffered(1))


def _silu(v):
    return v * jax.nn.sigmoid(v)


def _softplus(v):
    return jnp.maximum(v, 0.0) + jnp.log1p(jnp.exp(-jnp.abs(v)))


def _dot(a, b):
    return jnp.dot(a, b, preferred_element_type=F32)


def _mod_kernel(c_ref, w_ref, b_ref, o_ref):
    s = _silu(c_ref[...])
    o_ref[...] = jnp.dot(s, w_ref[...], precision=HIGHEST,
                         preferred_element_type=F32) + b_ref[...]


def _modulation(cond, w_ada, b_ada):
    rows = cond.shape[0]
    n = w_ada.shape[1]
    tn = 1024
    return pl.pallas_call(
        _mod_kernel,
        grid=(n // tn,),
        in_specs=[_const_spec((rows, D_MODEL)),
                  pl.BlockSpec((D_MODEL, tn), lambda j: (0, j)),
                  pl.BlockSpec((1, tn), lambda j: (0, j))],
        out_specs=pl.BlockSpec((rows, tn), lambda j: (0, j)),
        out_shape=jax.ShapeDtypeStruct((rows, n), F32),
        compiler_params=_params(1),
        name="modulation",
    )(cond, w_ada, b_ada.reshape(1, n))


XBC_SPLIT = 3
IN_PROJ_TILE = 1024


def _in_proj_kernel(tm, tps, x_ref, xp_ref, xn_ref, sc_ref, sh_ref, g_ref,
                    wz_ref, wx_ref, wdt_ref, wa_ref, wg_ref, bdt_ref, cw_ref, cb_ref,
                    z_ref, xbc_ref, dtf_ref, dtb_ref, dtft_ref, dtbt_ref, u_ref,
                    hb_ref, *ext_refs):
    t = pl.program_id(0)
    first = (t % tps) == 0
    last = (t % tps) == tps - 1
    scale = g_ref[...] * (1.0 + sc_ref[...])
    shift = sh_ref[...]

    def mod_norm(x):
        ms = jnp.mean(x * x, axis=-1, keepdims=True)
        return (x * lax.rsqrt(ms + EPS) * scale + shift).astype(BF16)

    zero = jnp.zeros((HALO, D_MODEL), BF16)
    hb_ref[0:HALO, :] = jnp.where(first, zero, mod_norm(xp_ref[...]))
    hb_ref[HALO:HALO + tm, :] = mod_norm(x_ref[...])
    hb_ref[HALO + tm:2 * HALO + tm, :] = jnp.where(last, zero, mod_norm(xn_ref[...]))
    hb = hb_ref[HALO:HALO + tm, :]

    piece = CONV_CH // XBC_SPLIT

    def project(p):
        ext_refs[p][...] = _dot(hb_ref[...], wx_ref[:, p * piece:(p + 1) * piece])

    def conv(p):
        cols = slice(p * piece, (p + 1) * piece)
        cw = cw_ref[:, cols]
        ext_ref = ext_refs[p]
        xc = (cw[1:2, :] * ext_ref[HALO:HALO + tm, :] + cb_ref[:, cols]
              + cw[0:1, :] * ext_ref[HALO - 1:HALO - 1 + tm, :]
              + cw[2:3, :] * ext_ref[HALO + 1:HALO + 1 + tm, :])
        xbc_ref[:, cols] = _silu(xc).astype(BF16)

    project(0)
    for p in range(1, XBC_SPLIT):
        project(p)
        conv(p - 1)
    z_ref[...] = _dot(hb, wz_ref[...]).astype(BF16)
    conv(XBC_SPLIT - 1)

    a = _dot(hb, wa_ref[...])
    g = _dot(hb, wg_ref[...])
    u_ref[...] = (a * jax.nn.sigmoid(g)).astype(BF16)
    dt = _softplus(_dot(hb, wdt_ref[...]) + bdt_ref[...])
    dtf, dtb = dt[:, :LANES], dt[:, LANES:]
    dtf_ref[...] = dtf
    dtb_ref[...] = dtb
    dtft_ref[...] = dtf.T[:SSD_HEADS, :]
    dtbt_ref[...] = dtb.T[:SSD_HEADS, :]


def _in_proj(x2d, seqlen, mod_row, tm, sc, sh, g, w):
    n = x2d.shape[0]
    hb = tm // HALO
    n_halo = n // HALO
    mod_spec = pl.BlockSpec((None, 1, D_MODEL), lambda i: (mod_row(i), 0, 0))
    tok = lambda width: pl.BlockSpec((tm, width), lambda i: (i, 0))
    tok_t = pl.BlockSpec((SSD_HEADS, tm), lambda i: (0, i))
    prev_spec = pl.BlockSpec((HALO, D_MODEL), lambda i: (jnp.maximum(i * hb - 1, 0), 0))
    next_spec = pl.BlockSpec((HALO, D_MODEL),
                             lambda i: (jnp.minimum((i + 1) * hb, n_halo - 1), 0))
    weights = [w["wz"], w["wx"], w["wdt"], w["wa"], w["wg"], w["bdt"], w["ssd_cw"], w["ssd_cb"]]
    return pl.pallas_call(
        functools.partial(_in_proj_kernel, tm, seqlen // tm),
        grid=(n // tm,),
        in_specs=[tok(D_MODEL), prev_spec, next_spec, mod_spec, mod_spec,
                  _const_spec((1, D_MODEL))]
                 + [_const_spec(a.shape) for a in weights],
        scratch_shapes=[pltpu.VMEM((tm + 2 * HALO, D_MODEL), BF16)]
                       + [pltpu.VMEM((tm + 2 * HALO, CONV_CH // XBC_SPLIT), F32)] * XBC_SPLIT,
        out_specs=[tok(SSD_WIDTH), tok(CONV_CH), tok(LANES), tok(LANES), tok_t, tok_t,
                   tok(CF_WIDTH)],
        out_shape=[jax.ShapeDtypeStruct((n, SSD_WIDTH), BF16),
                   jax.ShapeDtypeStruct((n, CONV_CH), BF16),
                   jax.ShapeDtypeStruct((n, LANES), F32),
                   jax.ShapeDtypeStruct((n, LANES), F32),
                   jax.ShapeDtypeStruct((SSD_HEADS, n), F32),
                   jax.ShapeDtypeStruct((SSD_HEADS, n), F32),
                   jax.ShapeDtypeStruct((n, CF_WIDTH), BF16)],
        compiler_params=_params(1),
        name="in_proj",
    )(x2d, x2d, x2d, sc, sh, g, *weights)


LOG2E = 1.4426950408889634
N_PAIRS = SSD_HEADS // 2
SSD_CHUNKS_PER_STEP = 4


def _ssd_kernel(fwd, zero_init, cps, *refs):
    refs = list(refs)
    xbc_ref, dt_ref, dtt_ref, alr_ref, alc_ref = refs[:5]
    del refs[:5]
    init_ref = None if zero_init else refs.pop(0)
    if fwd:
        yb_ref, z_ref, dskip_ref, nrm_ref = refs[:4]
        del refs[:4]
    y_ref, fin_ref = refs[:2]
    s_refs = refs[2:2 + N_PAIRS]
    yacc_refs = refs[2 + N_PAIRS:]
    j = pl.program_id(1)

    @pl.when(j == 0)
    def _():
        for pair in range(N_PAIRS):
            if zero_init:
                s_refs[pair][...] = jnp.zeros((D_STATE, LANES), F32)
            else:
                s_refs[pair][...] = init_ref[pair * LANES:(pair + 1) * LANES, :].T

    ri = lax.broadcasted_iota(jnp.int32, (CHUNK, CHUNK), 0)
    ci = lax.broadcasted_iota(jnp.int32, (CHUNK, CHUNK), 1)
    keep = (ri >= ci) if fwd else (ri <= ci)
    tri = keep.astype(F32)
    lane = lax.broadcasted_iota(jnp.int32, (1, LANES), 1)
    a_row = jnp.where(lane < SSD_HEADS, -jnp.exp(alr_ref[...]) * LOG2E, 0.0)
    a_col = -jnp.exp(alc_ref[...]) * LOG2E
    half = lane < SSD_HEAD_DIM
    nt = (((1,), (1,)), ((), ()))
    heads_per_group = SSD_HEADS // N_GROUPS
    end = CHUNK - 1 if fwd else 0

    pre = []
    for sc in range(cps):
        rows = slice(sc * CHUNK, (sc + 1) * CHUNK)
        dt_row = dtt_ref[:, rows]
        cum_col = jnp.dot(tri, dt_ref[rows, :] * a_row, precision=HIGHEST,
                          preferred_element_type=F32)
        cum_row = lax.dot_general(dt_row * a_col, tri, nt, precision=HIGHEST,
                                  preferred_element_type=F32)
        cum_end = cum_row[:, end:end + 1]
        p = {
            "cum_col": cum_col,
            "wgt_row": jnp.exp2(cum_end - cum_row) * dt_row,
            "edec": jnp.exp2(cum_end),
            "ecol": jnp.exp2(cum_col),
            "src_row": cum_row - jnp.log2(dt_row),
            "cb": [], "cg": [], "bgt": [],
        }
        for grp in range(N_GROUPS):
            b_cols = slice(SSD_WIDTH + grp * D_STATE, SSD_WIDTH + (grp + 1) * D_STATE)
            c_cols = slice(SSD_WIDTH + (N_GROUPS + grp) * D_STATE,
                           SSD_WIDTH + (N_GROUPS + grp + 1) * D_STATE)
            p["cb"].append(lax.dot_general(xbc_ref[rows, c_cols], xbc_ref[rows, b_cols], nt,
                                           preferred_element_type=F32))
            p["cg"].append(xbc_ref[rows, c_cols].astype(F32))
            p["bgt"].append(xbc_ref[rows, b_cols].astype(F32).T)
        pre.append(p)

    for sc in (range(cps) if fwd else reversed(range(cps))):
        rows = slice(sc * CHUNK, (sc + 1) * CHUNK)
        p = pre[sc]
        cum_col, wgt_row, edec, ecol, src_row = (
            p["cum_col"], p["wgt_row"], p["edec"], p["ecol"], p["src_row"])
        for grp in range(N_GROUPS):
            cb, cg, bgt = p["cb"][grp], p["cg"][grp], p["bgt"][grp]
            for pair in range(grp * heads_per_group // 2, (grp + 1) * heads_per_group // 2):
                lanes = slice(pair * LANES, (pair + 1) * LANES)
                xp = xbc_ref[rows, lanes]
                zero = jnp.zeros_like(xp)
                x_bd = jnp.concatenate(
                    [jnp.where(half, xp, zero), jnp.where(half, zero, xp)], axis=0)
                sp = s_refs[pair][...]
                spb = sp.astype(BF16)
                s_bd = jnp.concatenate(
                    [jnp.where(half, spb, zero), jnp.where(half, zero, spb)], axis=0)
                m_parts, c_parts, b_parts = [], [], []
                for h in (2 * pair, 2 * pair + 1):
                    seg = jnp.where(keep, cum_col[:, h:h + 1] - src_row[h:h + 1, :], -1e30)
                    m_parts.append((cb * jnp.exp2(seg)).astype(BF16))
                    c_parts.append((cg * ecol[:, h:h + 1]).astype(BF16))
                    b_parts.append((bgt * wgt_row[h:h + 1, :]).astype(BF16))
                lhs = jnp.concatenate(m_parts + c_parts, axis=1)
                rhs = jnp.concatenate([x_bd, s_bd], axis=0)
                y_pair = _dot(lhs, rhs)
                if fwd:
                    yacc_refs[sc * N_PAIRS + pair][...] = (
                        y_pair + xp.astype(F32) * dskip_ref[:, lanes])
                else:
                    y_ref[rows, lanes] = y_pair.astype(BF16)
                dec = jnp.where(half, edec[2 * pair:2 * pair + 1, :],
                                edec[2 * pair + 1:2 * pair + 2, :])
                s_refs[pair][...] = sp * dec + _dot(jnp.concatenate(b_parts, axis=1), x_bd)

        if fwd:
            yz = []
            for pair in range(N_PAIRS):
                lanes = slice(pair * LANES, (pair + 1) * LANES)
                y = yacc_refs[sc * N_PAIRS + pair][...] + yb_ref[rows, lanes].astype(F32)
                yz.append(y * _silu(z_ref[rows, lanes].astype(F32)))
            sq = yz[0] * yz[0]
            for v in yz[1:]:
                sq = sq + v * v
            rstd = lax.rsqrt(jnp.sum(sq, axis=-1, keepdims=True) * (1.0 / SSD_WIDTH) + EPS)
            for pair in range(N_PAIRS):
                lanes = slice(pair * LANES, (pair + 1) * LANES)
                y_ref[rows, lanes] = (yz[pair] * rstd * nrm_ref[:, lanes]).astype(BF16)

    @pl.when(j == pl.num_programs(1) - 1)
    def _():
        for pair in range(N_PAIRS):
            fin_ref[pair * LANES:(pair + 1) * LANES, :] = s_refs[pair][...].T


def _ssd_sweep(fwd, zero_init, bsz, seqlen, xbc, dt, dtt, alog, init, extra):
    cps = min(SSD_CHUNKS_PER_STEP, seqlen // CHUNK)
    blk = cps * CHUNK
    ns = seqlen // blk

    def block_of(j):
        return j if fwd else ns - 1 - j

    tok = lambda width: pl.BlockSpec(
        (blk, width), lambda b, j: (b * ns + block_of(j), 0))
    dtt_spec = pl.BlockSpec((SSD_HEADS, blk), lambda b, j: (0, b * ns + block_of(j)))
    state_spec = pl.BlockSpec((None, SSD_WIDTH, D_STATE), lambda b, j: (b, 0, 0))
    alog_row = jnp.pad(alog.reshape(1, SSD_HEADS), ((0, 0), (0, LANES - SSD_HEADS)))
    alog_col = alog.reshape(SSD_HEADS, 1)
    in_specs = [tok(CONV_CH), tok(LANES), dtt_spec,
                _const_spec((1, LANES)), _const_spec((SSD_HEADS, 1))]
    args = [xbc, dt, dtt, alog_row, alog_col]
    if not zero_init:
        in_specs.append(state_spec)
        args.append(init)
    if fwd:
        yb, z, dskip, nrm = extra
        in_specs += [tok(SSD_WIDTH), tok(SSD_WIDTH), _const_spec((1, SSD_WIDTH)),
                     _const_spec((1, SSD_WIDTH))]
        args += [yb, z, dskip, nrm]
    return pl.pallas_call(
        functools.partial(_ssd_kernel, fwd, zero_init, cps),
        grid=(bsz, ns),
        in_specs=in_specs,
        out_specs=[tok(SSD_WIDTH), state_spec],
        out_shape=[jax.ShapeDtypeStruct((bsz * seqlen, SSD_WIDTH), BF16),
                   jax.ShapeDtypeStruct((bsz, SSD_WIDTH, D_STATE), F32)],
        scratch_shapes=[pltpu.VMEM((D_STATE, LANES), F32)] * N_PAIRS
                       + [pltpu.VMEM((CHUNK, LANES), F32)] * (cps * N_PAIRS if fwd else 0),
        compiler_params=_params(2),
        name="ssd_fwd" if fwd else "ssd_bwd",
    )(*args)


def _cf_pitches(tm):
    seg = tm // SUBLANES
    in_pitch = -(-(seg + CF_KERNEL - 1 - 4) // 8) * 8 + 4
    out_pitch = seg + 8
    return seg, in_pitch, out_pitch


def _cf_kernel(tm, tps, u_ref, up_ref, un_ref, w_ref, b_ref, g_ref, beta_ref, o_ref,
               buf_ref, slab_ref, acc_ref):
    seg, in_pitch, out_pitch = _cf_pitches(tm)
    pad = (CF_KERNEL - 1) // 2
    t = pl.program_id(0)
    first = (t % tps) == 0
    last = (t % tps) == tps - 1
    buf_ref[0:HALO, :] = jnp.where(first, 0.0, up_ref[...].astype(F32))
    buf_ref[HALO:HALO + tm, :] = u_ref[...].astype(F32)
    buf_ref[HALO + tm:2 * HALO + tm, :] = jnp.where(last, 0.0, un_ref[...].astype(F32))
    span = seg + 2 * pad
    for jb in range(CF_WIDTH // LANES):
        for r in range(SUBLANES):
            src = HALO - pad + r * seg
            slab_ref[jb, r * in_pitch:r * in_pitch + span, :] = (
                buf_ref[src:src + span, jb * LANES:(jb + 1) * LANES])

    def lane_block(jb, carry):
        l0 = pl.multiple_of(jb * LANES, LANES)
        w = w_ref[:, pl.ds(l0, LANES)]
        taps = [jnp.broadcast_to(w[k:k + 1, :], (SUBLANES, LANES)) for k in range(CF_KERNEL)]
        bias = jnp.broadcast_to(b_ref[:, pl.ds(l0, LANES)], (SUBLANES, LANES))
        for i in range(seg):
            acc = bias
            for k in range(CF_KERNEL):
                acc = acc + taps[k] * slab_ref[jb, pl.ds(i + k, SUBLANES, stride=in_pitch), :]
            acc_ref[jb, pl.ds(i, SUBLANES, stride=out_pitch), :] = acc
        return carry

    lax.fori_loop(0, CF_WIDTH // LANES, lane_block, 0)

    nb = CF_WIDTH // LANES
    for r in range(SUBLANES):
        v = [acc_ref[jb, r * out_pitch:r * out_pitch + seg, :] for jb in range(nb)]
        mu = jnp.sum(sum(v[1:], v[0]), axis=-1, keepdims=True) * (1.0 / CF_WIDTH)
        d = [vj - mu for vj in v]
        sq = d[0] * d[0]
        for dj in d[1:]:
            sq = sq + dj * dj
        rstd = lax.rsqrt(jnp.sum(sq, axis=-1, keepdims=True) * (1.0 / CF_WIDTH) + EPS)
        for jb in range(nb):
            lanes = slice(jb * LANES, (jb + 1) * LANES)
            y = d[jb] * rstd * g_ref[:, lanes] + beta_ref[:, lanes]
            o_ref[r * seg:(r + 1) * seg, lanes] = _silu(y).astype(BF16)


def _cf_module(u, seqlen, tm, w, b, g, beta):
    n = u.shape[0]
    tps = seqlen // tm
    hb = tm // HALO
    n_halo = n // HALO
    _, in_pitch, out_pitch = _cf_pitches(tm)
    nb = CF_WIDTH // LANES
    return pl.pallas_call(
        functools.partial(_cf_kernel, tm, tps),
        grid=(n // tm,),
        in_specs=[pl.BlockSpec((tm, CF_WIDTH), lambda t: (t, 0)),
                  pl.BlockSpec((HALO, CF_WIDTH), lambda t: (jnp.maximum(t * hb - 1, 0), 0)),
                  pl.BlockSpec((HALO, CF_WIDTH),
                               lambda t: (jnp.minimum((t + 1) * hb, n_halo - 1), 0)),
                  _const_spec(w.shape), _const_spec(b.shape), _const_spec(g.shape),
                  _const_spec(beta.shape)],
        out_specs=pl.BlockSpec((tm, CF_WIDTH), lambda t: (t, 0)),
        out_shape=jax.ShapeDtypeStruct((n, CF_WIDTH), BF16),
        scratch_shapes=[pltpu.VMEM((tm + 2 * HALO, CF_WIDTH), F32),
                        pltpu.VMEM((nb, SUBLANES * in_pitch, LANES), F32),
                        pltpu.VMEM((nb, SUBLANES * out_pitch, LANES), F32)],
        compiler_params=_params(1),
        name="cf_module",
    )(u, u, u, w, b, g, beta)


MLP_KW = 256
GROUP = SUBLANES * SUBLANES


def _mlp_kernel(grid2d, nseg, tpi, *refs):
    refs = list(refs)
    y_ref, u_ref, x_ref = refs[:3]
    del refs[:3]
    if grid2d:
        yp_ref, up_ref, xp_ref, yn_ref, un_ref, xn_ref = refs[:6]
        del refs[:6]
    (g1_ref, sc_ref, sh_ref, g2_ref, wo1_ref, wo2_ref, gpost1_ref, gpre2_ref, wup_ref,
     cw_ref, cb_ref, wd_ref, gpost2_ref, o_ref) = refs[:14]
    yext_ref, uext_ref, xext_ref, hs_ref, hb_ref, x1_ref, blk_a, blk_b, act_ref, fs_ref = (
        refs[14:])
    tm = GROUP * nseg
    lo = GROUP if grid2d else 0
    ext = tm + 2 * lo
    nlb = D_MODEL // LANES
    t = pl.program_id(0)
    top = (t % tpi) == 0
    bottom = (t % tpi) == tpi - 1

    yext_ref[lo:lo + tm, :] = y_ref[...]
    uext_ref[lo:lo + tm, :] = u_ref[...]
    xext_ref[lo:lo + tm, :] = x_ref[...]
    if grid2d:
        yext_ref[0:lo, :] = yp_ref[...]
        uext_ref[0:lo, :] = up_ref[...]
        xext_ref[0:lo, :] = xp_ref[...]
        yext_ref[lo + tm:, :] = yn_ref[...]
        uext_ref[lo + tm:, :] = un_ref[...]
        xext_ref[lo + tm:, :] = xn_ref[...]
    mix = _dot(yext_ref[...], wo1_ref[...]) + _dot(uext_ref[...], wo2_ref[...])
    ms = jnp.mean(mix * mix, axis=-1, keepdims=True)
    x1 = xext_ref[...] + g1_ref[...] * (mix * lax.rsqrt(ms + EPS) * gpost1_ref[...])
    x1_ref[...] = x1[lo:lo + tm, :]
    ms2 = jnp.mean(x1 * x1, axis=-1, keepdims=True)
    h = x1 * lax.rsqrt(ms2 + EPS) * (gpre2_ref[...] * (1.0 + sc_ref[...])) + sh_ref[...]

    for j in range(nlb):
        hs_ref[j] = h[:, j * LANES:(j + 1) * LANES]
    for j in range(nlb):
        for g in range(ext // GROUP):
            grp = jnp.concatenate(
                [hs_ref[j, pl.ds(g * GROUP + i, SUBLANES, stride=SUBLANES), :]
                 for i in range(SUBLANES)], axis=0)
            if grid2d and g == 0:
                grp = jnp.where(top, 0.0, grp)
            if grid2d and g == ext // GROUP - 1:
                grp = jnp.where(bottom, 0.0, grp)
            hb_ref[g * GROUP:(g + 1) * GROUP, j * LANES:(j + 1) * LANES] = grp.astype(BF16)

    sub = lax.broadcasted_iota(jnp.int32, (SUBLANES, 1), 0)

    def shift_down(v, fill):
        edge = 0.0 if fill is None else pltpu.roll(fill, 1, axis=0)
        return jnp.where(sub > 0, pltpu.roll(v, 1, axis=0), edge)

    def shift_up(v, fill):
        edge = 0.0 if fill is None else pltpu.roll(fill, SUBLANES - 1, axis=0)
        return jnp.where(sub < SUBLANES - 1, pltpu.roll(v, SUBLANES - 1, axis=0), edge)

    def project(k0, blk):
        blk[:, 0:MLP_KW] = _dot(hb_ref[...], wup_ref[:, pl.ds(k0, MLP_KW)])
        blk[:, MLP_KW:2 * MLP_KW] = _dot(hb_ref[...], wup_ref[:, pl.ds(D_FF + k0, MLP_KW)])

    def conv_vreg(blk, s, i, cols, w, bias, loaded):
        def column(g, k, dc):
            def vreg(gg):
                if (gg, k) not in loaded:
                    r0 = gg * GROUP + k * SUBLANES
                    loaded[(gg, k)] = blk[r0:r0 + SUBLANES, cols]
                return loaded[(gg, k)]
            if grid2d:
                acc = w[dc:dc + 1, :] * vreg(g)
                for dr in (1, 2):
                    acc = acc + w[3 * dr + dc:3 * dr + dc + 1, :] * vreg(g + dr)
                return acc
            return w[3 + dc:4 + dc, :] * vreg(g)

        if i > 0:
            left = column(s, i - 1, 0)
        else:
            before = column(s - 1, SUBLANES - 1, 0) if (not grid2d and s > 0) else None
            left = shift_down(column(s, SUBLANES - 1, 0), before)
        if i < SUBLANES - 1:
            right = column(s, i + 1, 2)
        else:
            after = column(s + 1, 0, 2) if (not grid2d and s + 1 < nseg) else None
            right = shift_up(column(s, 0, 2), after)
        return column(s, i, 1) + bias + left + right

    def gate_slice(k0, blk):
        for q in range(MLP_KW // LANES):
            lg = pl.multiple_of(k0 + q * LANES, LANES)
            lv = pl.multiple_of(D_FF + k0 + q * LANES, LANES)
            wg, wv = cw_ref[:, pl.ds(lg, LANES)], cw_ref[:, pl.ds(lv, LANES)]
            bg, bv = cb_ref[:, pl.ds(lg, LANES)], cb_ref[:, pl.ds(lv, LANES)]
            cg = slice(q * LANES, (q + 1) * LANES)
            cv = slice(MLP_KW + q * LANES, MLP_KW + (q + 1) * LANES)
            gate_in, val_in = {}, {}
            for s in range(nseg):
                for i0 in range(0, SUBLANES, 2):
                    act = jnp.concatenate(
                        [_silu(conv_vreg(blk, s, i, cg, wg, bg, gate_in))
                         * conv_vreg(blk, s, i, cv, wv, bv, val_in) for i in (i0, i0 + 1)],
                        axis=0)
                    r0 = s * GROUP + i0 * SUBLANES
                    act_ref[r0:r0 + 2 * SUBLANES, pl.ds(lg, LANES)] = act.astype(BF16)

    def slice_pair(jj, carry):
        k0 = pl.multiple_of(jj * (2 * MLP_KW), 2 * MLP_KW)
        project(k0 + MLP_KW, blk_b)
        gate_slice(k0, blk_a)
        project(k0 + 2 * MLP_KW, blk_a)
        gate_slice(k0 + MLP_KW, blk_b)
        return carry

    n_slices = D_FF // MLP_KW
    assert n_slices % 2 == 1 and n_slices * MLP_KW == D_FF
    project(0, blk_a)
    lax.fori_loop(0, (n_slices - 1) // 2, slice_pair, 0)
    gate_slice((n_slices - 1) * MLP_KW, blk_a)

    f = _dot(act_ref[...], wd_ref[...])
    msf = jnp.mean(f * f, axis=-1, keepdims=True)
    fn = f * lax.rsqrt(msf + EPS) * gpost2_ref[...]
    for j in range(nlb):
        for g in range(nseg):
            for i in range(SUBLANES):
                r0 = g * GROUP + i * SUBLANES
                fs_ref[j, pl.ds(g * GROUP + i, SUBLANES, stride=SUBLANES), :] = (
                    fn[r0:r0 + SUBLANES, j * LANES:(j + 1) * LANES])
    for j in range(nlb):
        lanes = slice(j * LANES, (j + 1) * LANES)
        o_ref[:, lanes] = x1_ref[:, lanes] + g2_ref[:, lanes] * fs_ref[j]


def _mlp(y, u, x2d, mod_row, grid2d, nseg, tpi, g1, sc, sh, g2, w):
    n = x2d.shape[0]
    seg = GROUP
    tm = seg * nseg
    ext = tm + 2 * seg if grid2d else tm
    nlb = D_MODEL // LANES
    n_seg_total = n // seg
    mod_spec = pl.BlockSpec((None, 1, D_MODEL), lambda i: (mod_row(i), 0, 0))
    tok = pl.BlockSpec((tm, D_MODEL), lambda i: (i, 0))
    prev_spec = pl.BlockSpec((seg, D_MODEL), lambda i: (jnp.maximum(i * nseg - 1, 0), 0))
    next_spec = pl.BlockSpec(
        (seg, D_MODEL), lambda i: (jnp.minimum((i + 1) * nseg, n_seg_total - 1), 0))
    halo_specs = [prev_spec] * 3 + [next_spec] * 3 if grid2d else []
    halo_args = [y, u, x2d, y, u, x2d] if grid2d else []
    weights = [w["wo1"], w["wo2"], w["gpost1"], w["gpre2"], w["wup"], w["ffn_cw"],
               w["ffn_cb"], w["wd"], w["gpost2"]]
    return pl.pallas_call(
        functools.partial(_mlp_kernel, grid2d, nseg, tpi),
        grid=(n // tm,),
        in_specs=[tok, tok, tok] + halo_specs + [mod_spec] * 4
                 + [_const_spec(a.shape) for a in weights],
        out_specs=tok,
        out_shape=jax.ShapeDtypeStruct((n, D_MODEL), F32),
        scratch_shapes=[pltpu.VMEM((ext, D_MODEL), BF16),
                        pltpu.VMEM((ext, D_MODEL), BF16),
                        pltpu.VMEM((ext, D_MODEL), F32),
                        pltpu.VMEM((nlb, ext, LANES), F32),
                        pltpu.VMEM((ext, D_MODEL), BF16),
                        pltpu.VMEM((tm, D_MODEL), F32),
                        pltpu.VMEM((ext, 2 * MLP_KW), F32),
                        pltpu.VMEM((ext, 2 * MLP_KW), F32),
                        pltpu.VMEM((tm, D_FF), BF16),
                        pltpu.VMEM((nlb, tm, LANES), F32)],
        compiler_params=_params(1),
        name="mlp_grid" if grid2d else "mlp_seq",
    )(y, u, x2d, *halo_args, g1, sc, sh, g2, *weights)


def _state_to_kernel_layout(s):
    return s.reshape(s.shape[0], SSD_WIDTH, D_STATE)


def _state_from_kernel_layout(s):
    return s.reshape(s.shape[0], SSD_HEADS, SSD_HEAD_DIM, D_STATE)


def _trunk_layer(x, mod, mod_row_of_batch, w, init_f, init_b, rows):
    bsz, seqlen, _ = x.shape
    x2d = x.reshape(bsz * seqlen, D_MODEL)
    sh1, sc1, g1, sh2, sc2, g2 = mod
    latent = rows is not None
    tm = 512 if latent else seqlen
    tps = seqlen // tm
    mod_row = lambda i: mod_row_of_batch(i // tps)

    tm_in = min(IN_PROJ_TILE, seqlen)
    z, xbc, dtf, dtb, dtft, dtbt, u = _in_proj(
        x2d, seqlen, lambda i: mod_row_of_batch(i // (seqlen // tm_in)), tm_in, sc1, sh1,
        w["gpre1"], w)

    zero_init = init_f is None
    yb, fin_b = _ssd_sweep(False, zero_init, bsz, seqlen, xbc, dtb, dtbt, w["alog_b"],
                           init_b, None)
    y, fin_f = _ssd_sweep(True, zero_init, bsz, seqlen, xbc, dtf, dtft, w["alog_f"],
                          init_f, (yb, z, w["dskip"], w["ssd_norm"]))

    ucf = _cf_module(u, seqlen, tm, w["cf_w"], w["cf_b"], w["cf_g"], w["cf_beta"])
    nseg = tm // GROUP
    if latent:
        assert GRID_W == GROUP
        out = _mlp(y, ucf, x2d, mod_row, True, nseg, rows // nseg, g1, sc2, sh2, g2, w)
    else:
        out = _mlp(y, ucf, x2d, mod_row, False, nseg, 1, g1, sc2, sh2, g2, w)
    return out.reshape(bsz, seqlen, D_MODEL), fin_f, fin_b


def _layer_weights(l, w_in, w_ssd_conv, b_ssd_conv, a_log_fwd, a_log_bwd, dt_bias_fwd,
                   dt_bias_bwd, d_skip, ssd_norm, w_cf_conv, b_cf_conv, cf_ln_g, cf_ln_b,
                   w_out, norm_mix_pre, norm_mix_post, norm_ffn_pre, norm_ffn_post,
                   w_ffn_up, w_ffn_conv, b_ffn_conv, w_ffn_down):
    wi = w_in[l].astype(BF16)
    o = 0
    wz = wi[:, o:o + SSD_WIDTH]; o += SSD_WIDTH
    wx = wi[:, o:o + CONV_CH]; o += CONV_CH
    wdf = wi[:, o:o + SSD_HEADS]; o += SSD_HEADS
    wdb = wi[:, o:o + SSD_HEADS]; o += SSD_HEADS
    wa = wi[:, o:o + CF_WIDTH]; o += CF_WIDTH
    wg = wi[:, o:o + CF_WIDTH]
    pad_l = lambda a: jnp.pad(a, ((0, 0), (0, LANES - a.shape[1])))
    row = lambda a: a.reshape(1, -1)
    wo = w_out[l].astype(BF16)
    cf_w = jnp.pad(w_cf_conv[l], ((0, 32 - CF_KERNEL), (0, 0)))
    ffn_cw = jnp.pad(w_ffn_conv[l].reshape(9, 2 * D_FF), ((0, 7), (0, 0)))
    return {
        "wz": wz, "wx": wx, "wdt": jnp.concatenate([pad_l(wdf), pad_l(wdb)], axis=1),
        "wa": wa, "wg": wg,
        "bdt": jnp.concatenate([pad_l(row(dt_bias_fwd[l])), pad_l(row(dt_bias_bwd[l]))],
                               axis=1),
        "gpre1": row(norm_mix_pre[l]), "gpost1": row(norm_mix_post[l]),
        "gpre2": row(norm_ffn_pre[l]), "gpost2": row(norm_ffn_post[l]),
        "alog_f": a_log_fwd[l], "alog_b": a_log_bwd[l],
        "ssd_cw": jnp.pad(w_ssd_conv[l], ((0, 5), (0, 0))), "ssd_cb": row(b_ssd_conv[l]),
        "dskip": row(jnp.repeat(d_skip[l], SSD_HEAD_DIM)), "ssd_norm": row(ssd_norm[l]),
        "cf_w": cf_w, "cf_b": row(b_cf_conv[l]), "cf_g": row(cf_ln_g[l]),
        "cf_beta": row(cf_ln_b[l]),
        "wo1": wo[:SSD_WIDTH], "wo2": wo[SSD_WIDTH:],
        "wup": w_ffn_up[l].astype(BF16), "ffn_cw": ffn_cw, "ffn_cb": row(b_ffn_conv[l]),
        "wd": w_ffn_down[l].astype(BF16),
    }


def kernel(x_prompt, x_sample, state_ssd_fwd, state_ssd_bwd, c, c_ctx, w_ada, b_ada, norm_mix_pre, norm_mix_post, w_in, w_ssd_conv, b_ssd_conv, a_log_fwd, a_log_bwd, dt_bias_fwd, dt_bias_bwd, d_skip, ssd_norm, w_cf_conv, b_cf_conv, cf_ln_g, cf_ln_b, w_out, norm_ffn_pre, norm_ffn_post, w_ffn_up, w_ffn_conv, b_ffn_conv, w_ffn_down):
    depth = w_ada.shape[0]
    dec_batch = x_sample.shape[0]
    rows = x_sample.shape[1] // GRID_W
    ctx_row = dec_batch
    n_cond = -(-(dec_batch + 1) // 8) * 8
    cond = jnp.zeros((n_cond, D_MODEL), F32)
    cond = cond.at[:dec_batch].set(c).at[ctx_row].set(c_ctx)

    xp, xl = x_prompt, x_sample
    new_f, new_b = [], []
    for l in range(depth):
        w = _layer_weights(l, w_in, w_ssd_conv, b_ssd_conv, a_log_fwd, a_log_bwd,
                           dt_bias_fwd, dt_bias_bwd, d_skip, ssd_norm, w_cf_conv, b_cf_conv,
                           cf_ln_g, cf_ln_b, w_out, norm_mix_pre, norm_mix_post,
                           norm_ffn_pre, norm_ffn_post, w_ffn_up, w_ffn_conv, b_ffn_conv,
                           w_ffn_down)
        mod = _modulation(cond, w_ada[l], b_ada[l])
        mod = [m.reshape(n_cond, 1, D_MODEL) for m in jnp.split(mod, 6, axis=-1)]
        xp, s_f, s_b = _trunk_layer(xp, mod, lambda b: ctx_row, w, None, None, None)
        new_f.append(_state_from_kernel_layout(s_f))
        new_b.append(_state_from_kernel_layout(s_b))
        xl, _, _ = _trunk_layer(xl, mod, lambda b: b, w,
                                _state_to_kernel_layout(state_ssd_fwd[:, l]),
                                _state_to_kernel_layout(state_ssd_bwd[:, l]), rows)
    return (xp, xl, jnp.stack(new_f, axis=1), jnp.stack(new_b, axis=1))
```

```python
import functools

import jax
import jax.numpy as jnp
from jax import lax
from jax.experimental import pallas as pl
from jax.experimental.pallas import tpu as pltpu

D_MODEL = 1024
GRID_W = 64
SSD_WIDTH = 1024
SSD_HEAD_DIM = 64
SSD_HEADS = 16
N_GROUPS = 2
D_STATE = 128
CHUNK = 128
CONV_CH = SSD_WIDTH + 2 * N_GROUPS * D_STATE
CF_WIDTH = 1024
CF_KERNEL = 31
D_FF = 2816
EPS = 1e-6

LANES = 128
SUBLANES = 8
HALO = 16
VMEM_LIMIT = 56 * 1024 * 1024

F32 = jnp.float32
BF16 = jnp.bfloat16
HIGHEST = lax.Precision.HIGHEST


def _params(n_axes):
    return pltpu.CompilerParams(
        dimension_semantics=("arbitrary",) * n_axes, vmem_limit_bytes=VMEM_LIMIT)


def _const_spec(shape):
    nd = len(shape)
    return pl.BlockSpec(shape, lambda *_: (0,) * nd, pipeline_mode=pl.Buffered(1))


def _silu(v):
    return v * jax.nn.sigmoid(v)


def _softplus(v):
    return jnp.maximum(v, 0.0) + jnp.log1p(jnp.exp(-jnp.abs(v)))


def _dot(a, b):
    return jnp.dot(a, b, preferred_element_type=F32)


def _mod_kernel(c_ref, w_ref, b_ref, o_ref):
    s = _silu(c_ref[...])
    o_ref[...] = jnp.dot(s, w_ref[...], precision=HIGHEST,
                         preferred_element_type=F32) + b_ref[...]


def _modulation(cond, w_ada, b_ada):
    rows = cond.shape[0]
    n = w_ada.shape[1]
    tn = 1024
    return pl.pallas_call(
        _mod_kernel,
        grid=(n // tn,),
        in_specs=[_const_spec((rows, D_MODEL)),
                  pl.BlockSpec((D_MODEL, tn), lambda j: (0, j)),
                  pl.BlockSpec((1, tn), lambda j: (0, j))],
        out_specs=pl.BlockSpec((rows, tn), lambda j: (0, j)),
        out_shape=jax.ShapeDtypeStruct((rows, n), F32),
        compiler_params=_params(1),
        name="modulation",
    )(cond, w_ada, b_ada.reshape(1, n))


XBC_SPLIT = 3
IN_PROJ_TILE = 1024


def _in_proj_kernel(tm, tps, x_ref, xp_ref, xn_ref, sc_ref, sh_ref, g_ref,
                    wz_ref, wx_ref, wdt_ref, wa_ref, wg_ref, bdt_ref, cw_ref, cb_ref,
                    z_ref, xbc_ref, dtf_ref, dtb_ref, dtft_ref, dtbt_ref, u_ref,
                    hb_ref, *ext_refs):
    t = pl.program_id(0)
    first = (t % tps) == 0
    last = (t % tps) == tps - 1
    scale = g_ref[...] * (1.0 + sc_ref[...])
    shift = sh_ref[...]

    def mod_norm(x):
        ms = jnp.mean(x * x, axis=-1, keepdims=True)
        return (x * lax.rsqrt(ms + EPS) * scale + shift).astype(BF16)

    zero = jnp.zeros((HALO, D_MODEL), BF16)
    hb_ref[0:HALO, :] = jnp.where(first, zero, mod_norm(xp_ref[...]))
    hb_ref[HALO:HALO + tm, :] = mod_norm(x_ref[...])
    hb_ref[HALO + tm:2 * HALO + tm, :] = jnp.where(last, zero, mod_norm(xn_ref[...]))
    hb = hb_ref[HALO:HALO + tm, :]

    piece = CONV_CH // XBC_SPLIT

    def project(p):
        ext_refs[p][...] = _dot(hb_ref[...], wx_ref[:, p * piece:(p + 1) * piece])

    def conv(p):
        cols = slice(p * piece, (p + 1) * piece)
        cw = cw_ref[:, cols]
        ext_ref = ext_refs[p]
        xc = (cw[1:2, :] * ext_ref[HALO:HALO + tm, :] + cb_ref[:, cols]
              + cw[0:1, :] * ext_ref[HALO - 1:HALO - 1 + tm, :]
              + cw[2:3, :] * ext_ref[HALO + 1:HALO + 1 + tm, :])
        xbc_ref[:, cols] = _silu(xc).astype(BF16)

    project(0)
    for p in range(1, XBC_SPLIT):
        project(p)
        conv(p - 1)
    z_ref[...] = _dot(hb, wz_ref[...]).astype(BF16)
    conv(XBC_SPLIT - 1)

    a = _dot(hb, wa_ref[...])
    g = _dot(hb, wg_ref[...])
    u_ref[...] = (a * jax.nn.sigmoid(g)).astype(BF16)
    dt = _softplus(_dot(hb, wdt_ref[...]) + bdt_ref[...])
    dtf, dtb = dt[:, :LANES], dt[:, LANES:]
    dtf_ref[...] = dtf
    dtb_ref[...] = dtb
    dtft_ref[...] = dtf.T[:SSD_HEADS, :]
    dtbt_ref[...] = dtb.T[:SSD_HEADS, :]


def _in_proj(x2d, seqlen, mod_row, tm, sc, sh, g, w):
    n = x2d.shape[0]
    hb = tm // HALO
    n_halo = n // HALO
    mod_spec = pl.BlockSpec((None, 1, D_MODEL), lambda i: (mod_row(i), 0, 0))
    tok = lambda width: pl.BlockSpec((tm, width), lambda i: (i, 0))
    tok_t = pl.BlockSpec((SSD_HEADS, tm), lambda i: (0, i))
    prev_spec = pl.BlockSpec((HALO, D_MODEL), lambda i: (jnp.maximum(i * hb - 1, 0), 0))
    next_spec = pl.BlockSpec((HALO, D_MODEL),
                             lambda i: (jnp.minimum((i + 1) * hb, n_halo - 1), 0))
    weights = [w["wz"], w["wx"], w["wdt"], w["wa"], w["wg"], w["bdt"], w["ssd_cw"], w["ssd_cb"]]
    return pl.pallas_call(
        functools.partial(_in_proj_kernel, tm, seqlen // tm),
        grid=(n // tm,),
        in_specs=[tok(D_MODEL), prev_spec, next_spec, mod_spec, mod_spec,
                  _const_spec((1, D_MODEL))]
                 + [_const_spec(a.shape) for a in weights],
        scratch_shapes=[pltpu.VMEM((tm + 2 * HALO, D_MODEL), BF16)]
                       + [pltpu.VMEM((tm + 2 * HALO, CONV_CH // XBC_SPLIT), F32)] * XBC_SPLIT,
        out_specs=[tok(SSD_WIDTH), tok(CONV_CH), tok(LANES), tok(LANES), tok_t, tok_t,
                   tok(CF_WIDTH)],
        out_shape=[jax.ShapeDtypeStruct((n, SSD_WIDTH), BF16),
                   jax.ShapeDtypeStruct((n, CONV_CH), BF16),
                   jax.ShapeDtypeStruct((n, LANES), F32),
                   jax.ShapeDtypeStruct((n, LANES), F32),
                   jax.ShapeDtypeStruct((SSD_HEADS, n), F32),
                   jax.ShapeDtypeStruct((SSD_HEADS, n), F32),
                   jax.ShapeDtypeStruct((n, CF_WIDTH), BF16)],
        compiler_params=_params(1),
        name="in_proj",
    )(x2d, x2d, x2d, sc, sh, g, *weights)


LOG2E = 1.4426950408889634
N_PAIRS = SSD_HEADS // 2
SSD_CHUNKS_PER_STEP = 4


def _ssd_kernel(fwd, zero_init, cps, *refs):
    refs = list(refs)
    xbc_ref, dt_ref, dtt_ref, alr_ref, alc_ref = refs[:5]
    del refs[:5]
    init_ref = None if zero_init else refs.pop(0)
    if fwd:
        yb_ref, z_ref, dskip_ref, nrm_ref = refs[:4]
        del refs[:4]
    y_ref, fin_ref = refs[:2]
    s_refs = refs[2:2 + N_PAIRS]
    yacc_refs = refs[2 + N_PAIRS:]
    j = pl.program_id(1)

    @pl.when(j == 0)
    def _():
        for pair in range(N_PAIRS):
            if zero_init:
                s_refs[pair][...] = jnp.zeros((D_STATE, LANES), F32)
            else:
                s_refs[pair][...] = init_ref[pair * LANES:(pair + 1) * LANES, :].T

    ri = lax.broadcasted_iota(jnp.int32, (CHUNK, CHUNK), 0)
    ci = lax.broadcasted_iota(jnp.int32, (CHUNK, CHUNK), 1)
    keep = (ri >= ci) if fwd else (ri <= ci)
    tri = keep.astype(F32)
    lane = lax.broadcasted_iota(jnp.int32, (1, LANES), 1)
    a_row = jnp.where(lane < SSD_HEADS, -jnp.exp(alr_ref[...]) * LOG2E, 0.0)
    a_col = -jnp.exp(alc_ref[...]) * LOG2E
    half = lane < SSD_HEAD_DIM
    nt = (((1,), (1,)), ((), ()))
    heads_per_group = SSD_HEADS // N_GROUPS
    end = CHUNK - 1 if fwd else 0

    pre = []
    for sc in range(cps):
        rows = slice(sc * CHUNK, (sc + 1) * CHUNK)
        dt_row = dtt_ref[:, rows]
        cum_col = jnp.dot(tri, dt_ref[rows, :] * a_row, precision=HIGHEST,
                          preferred_element_type=F32)
        cum_row = lax.dot_general(dt_row * a_col, tri, nt, precision=HIGHEST,
                                  preferred_element_type=F32)
        cum_end = cum_row[:, end:end + 1]
        p = {
            "cum_col": cum_col,
            "wgt_row": jnp.exp2(cum_end - cum_row) * dt_row,
            "edec": jnp.exp2(cum_end),
            "ecol": jnp.exp2(cum_col),
            "src_row": cum_row - jnp.log2(dt_row),
            "cb": [], "cg": [], "bgt": [],
        }
        for grp in range(N_GROUPS):
            b_cols = slice(SSD_WIDTH + grp * D_STATE, SSD_WIDTH + (grp + 1) * D_STATE)
            c_cols = slice(SSD_WIDTH + (N_GROUPS + grp) * D_STATE,
                           SSD_WIDTH + (N_GROUPS + grp + 1) * D_STATE)
            p["cb"].append(lax.dot_general(xbc_ref[rows, c_cols], xbc_ref[rows, b_cols], nt,
                                           preferred_element_type=F32))
            p["cg"].append(xbc_ref[rows, c_cols].astype(F32))
            p["bgt"].append(xbc_ref[rows, b_cols].astype(F32).T)
        pre.append(p)

    for sc in (range(cps) if fwd else reversed(range(cps))):
        rows = slice(sc * CHUNK, (sc + 1) * CHUNK)
        p = pre[sc]
        cum_col, wgt_row, edec, ecol, src_row = (
            p["cum_col"], p["wgt_row"], p["edec"], p["ecol"], p["src_row"])
        for grp in range(N_GROUPS):
            cb, cg, bgt = p["cb"][grp], p["cg"][grp], p["bgt"][grp]
            for pair in range(grp * heads_per_group // 2, (grp + 1) * heads_per_group // 2):
                lanes = slice(pair * LANES, (pair + 1) * LANES)
                xp = xbc_ref[rows, lanes]
                zero = jnp.zeros_like(xp)
                x_bd = jnp.concatenate(
                    [jnp.where(half, xp, zero), jnp.where(half, zero, xp)], axis=0)
                sp = s_refs[pair][...]
                spb = sp.astype(BF16)
                s_bd = jnp.concatenate(
                    [jnp.where(half, spb, zero), jnp.where(half, zero, spb)], axis=0)
                m_parts, c_parts, b_parts = [], [], []
                for h in (2 * pair, 2 * pair + 1):
                    seg = jnp.where(keep, cum_col[:, h:h + 1] - src_row[h:h + 1, :], -1e30)
                    m_parts.append((cb * jnp.exp2(seg)).astype(BF16))
                    c_parts.append((cg * ecol[:, h:h + 1]).astype(BF16))
                    b_parts.append((bgt * wgt_row[h:h + 1, :]).astype(BF16))
                lhs = jnp.concatenate(m_parts + c_parts, axis=1)
                rhs = jnp.concatenate([x_bd, s_bd], axis=0)
                y_pair = _dot(lhs, rhs)
                if fwd:
                    yacc_refs[sc * N_PAIRS + pair][...] = (
                        y_pair + xp.astype(F32) * dskip_ref[:, lanes])
                else:
                    y_ref[rows, lanes] = y_pair.astype(BF16)
                dec = jnp.where(half, edec[2 * pair:2 * pair + 1, :],
                                edec[2 * pair + 1:2 * pair + 2, :])
                s_refs[pair][...] = sp * dec + _dot(jnp.concatenate(b_parts, axis=1), x_bd)

        if fwd:
            yz = []
            for pair in range(N_PAIRS):
                lanes = slice(pair * LANES, (pair + 1) * LANES)
                y = yacc_refs[sc * N_PAIRS + pair][...] + yb_ref[rows, lanes].astype(F32)
                yz.append(y * _silu(z_ref[rows, lanes].astype(F32)))
            sq = yz[0] * yz[0]
            for v in yz[1:]:
                sq = sq + v * v
            rstd = lax.rsqrt(jnp.sum(sq, axis=-1, keepdims=True) * (1.0 / SSD_WIDTH) + EPS)
            for pair in range(N_PAIRS):
                lanes = slice(pair * LANES, (pair + 1) * LANES)
                y_ref[rows, lanes] = (yz[pair] * rstd * nrm_ref[:, lanes]).astype(BF16)

    @pl.when(j == pl.num_programs(1) - 1)
    def _():
        for pair in range(N_PAIRS):
            fin_ref[pair * LANES:(pair + 1) * LANES, :] = s_refs[pair][...].T


def _ssd_sweep(fwd, zero_init, bsz, seqlen, xbc, dt, dtt, alog, init, extra):
    cps = min(SSD_CHUNKS_PER_STEP, seqlen // CHUNK)
    blk = cps * CHUNK
    ns = seqlen // blk

    def block_of(j):
        return j if fwd else ns - 1 - j

    tok = lambda width: pl.BlockSpec(
        (blk, width), lambda b, j: (b * ns + block_of(j), 0))
    dtt_spec = pl.BlockSpec((SSD_HEADS, blk), lambda b, j: (0, b * ns + block_of(j)))
    state_spec = pl.BlockSpec((None, SSD_WIDTH, D_STATE), lambda b, j: (b, 0, 0))
    alog_row = jnp.pad(alog.reshape(1, SSD_HEADS), ((0, 0), (0, LANES - SSD_HEADS)))
    alog_col = alog.reshape(SSD_HEADS, 1)
    in_specs = [tok(CONV_CH), tok(LANES), dtt_spec,
                _const_spec((1, LANES)), _const_spec((SSD_HEADS, 1))]
    args = [xbc, dt, dtt, alog_row, alog_col]
    if not zero_init:
        in_specs.append(state_spec)
        args.append(init)
    if fwd:
        yb, z, dskip, nrm = extra
        in_specs += [tok(SSD_WIDTH), tok(SSD_WIDTH), _const_spec((1, SSD_WIDTH)),
                     _const_spec((1, SSD_WIDTH))]
        args += [yb, z, dskip, nrm]
    return pl.pallas_call(
        functools.partial(_ssd_kernel, fwd, zero_init, cps),
        grid=(bsz, ns),
        in_specs=in_specs,
        out_specs=[tok(SSD_WIDTH), state_spec],
        out_shape=[jax.ShapeDtypeStruct((bsz * seqlen, SSD_WIDTH), BF16),
                   jax.ShapeDtypeStruct((bsz, SSD_WIDTH, D_STATE), F32)],
        scratch_shapes=[pltpu.VMEM((D_STATE, LANES), F32)] * N_PAIRS
                       + [pltpu.VMEM((CHUNK, LANES), F32)] * (cps * N_PAIRS if fwd else 0),
        compiler_params=_params(2),
        name="ssd_fwd" if fwd else "ssd_bwd",
    )(*args)


def _cf_pitches(tm):
    seg = tm // SUBLANES
    in_pitch = -(-(seg + CF_KERNEL - 1 - 4) // 8) * 8 + 4
    out_pitch = seg + 8
    return seg, in_pitch, out_pitch


def _cf_kernel(tm, tps, u_ref, up_ref, un_ref, w_ref, b_ref, g_ref, beta_ref, o_ref,
               buf_ref, slab_ref, acc_ref):
    seg, in_pitch, out_pitch = _cf_pitches(tm)
    pad = (CF_KERNEL - 1) // 2
    t = pl.program_id(0)
    first = (t % tps) == 0
    last = (t % tps) == tps - 1
    buf_ref[0:HALO, :] = jnp.where(first, 0.0, up_ref[...].astype(F32))
    buf_ref[HALO:HALO + tm, :] = u_ref[...].astype(F32)
    buf_ref[HALO + tm:2 * HALO + tm, :] = jnp.where(last, 0.0, un_ref[...].astype(F32))
    span = seg + 2 * pad
    for jb in range(CF_WIDTH // LANES):
        for r in range(SUBLANES):
            src = HALO - pad + r * seg
            slab_ref[jb, r * in_pitch:r * in_pitch + span, :] = (
                buf_ref[src:src + span, jb * LANES:(jb + 1) * LANES])

    def lane_block(jb, carry):
        l0 = pl.multiple_of(jb * LANES, LANES)
        w = w_ref[:, pl.ds(l0, LANES)]
        taps = [jnp.broadcast_to(w[k:k + 1, :], (SUBLANES, LANES)) for k in range(CF_KERNEL)]
        bias = jnp.broadcast_to(b_ref[:, pl.ds(l0, LANES)], (SUBLANES, LANES))
        for i in range(seg):
            acc = bias
            for k in range(CF_KERNEL):
                acc = acc + taps[k] * slab_ref[jb, pl.ds(i + k, SUBLANES, stride=in_pitch), :]
            acc_ref[jb, pl.ds(i, SUBLANES, stride=out_pitch), :] = acc
        return carry

    lax.fori_loop(0, CF_WIDTH // LANES, lane_block, 0)

    nb = CF_WIDTH // LANES
    for r in range(SUBLANES):
        v = [acc_ref[jb, r * out_pitch:r * out_pitch + seg, :] for jb in range(nb)]
        mu = jnp.sum(sum(v[1:], v[0]), axis=-1, keepdims=True) * (1.0 / CF_WIDTH)
        d = [vj - mu for vj in v]
        sq = d[0] * d[0]
        for dj in d[1:]:
            sq = sq + dj * dj
        rstd = lax.rsqrt(jnp.sum(sq, axis=-1, keepdims=True) * (1.0 / CF_WIDTH) + EPS)
        for jb in range(nb):
            lanes = slice(jb * LANES, (jb + 1) * LANES)
            y = d[jb] * rstd * g_ref[:, lanes] + beta_ref[:, lanes]
            o_ref[r * seg:(r + 1) * seg, lanes] = _silu(y).astype(BF16)


def _cf_module(u, seqlen, tm, w, b, g, beta):
    n = u.shape[0]
    tps = seqlen // tm
    hb = tm // HALO
    n_halo = n // HALO
    _, in_pitch, out_pitch = _cf_pitches(tm)
    nb = CF_WIDTH // LANES
    return pl.pallas_call(
        functools.partial(_cf_kernel, tm, tps),
        grid=(n // tm,),
        in_specs=[pl.BlockSpec((tm, CF_WIDTH), lambda t: (t, 0)),
                  pl.BlockSpec((HALO, CF_WIDTH), lambda t: (jnp.maximum(t * hb - 1, 0), 0)),
                  pl.BlockSpec((HALO, CF_WIDTH),
                               lambda t: (jnp.minimum((t + 1) * hb, n_halo - 1), 0)),
                  _const_spec(w.shape), _const_spec(b.shape), _const_spec(g.shape),
                  _const_spec(beta.shape)],
        out_specs=pl.BlockSpec((tm, CF_WIDTH), lambda t: (t, 0)),
        out_shape=jax.ShapeDtypeStruct((n, CF_WIDTH), BF16),
        scratch_shapes=[pltpu.VMEM((tm + 2 * HALO, CF_WIDTH), F32),
                        pltpu.VMEM((nb, SUBLANES * in_pitch, LANES), F32),
                        pltpu.VMEM((nb, SUBLANES * out_pitch, LANES), F32)],
        compiler_params=_params(1),
        name="cf_module",
    )(u, u, u, w, b, g, beta)


MLP_KW = 256
MLP_TILE = 512
CF_TILE = 1024
GROUP = SUBLANES * SUBLANES


def _mlp_kernel(grid2d, nseg, tpi, *refs):
    refs = list(refs)
    y_ref, u_ref, x_ref = refs[:3]
    del refs[:3]
    if grid2d:
        yp_ref, up_ref, xp_ref, yn_ref, un_ref, xn_ref = refs[:6]
        del refs[:6]
    (g1_ref, sc_ref, sh_ref, g2_ref, wo_ref, gpost1_ref, gpre2_ref, wup_ref,
     cw_ref, cb_ref, wd_ref, gpost2_ref, o_ref) = refs[:13]
    yu_ref, xext_ref, hs_ref, hb_ref, x1_ref, blk_a, blk_b, act_ref, fs_ref = refs[13:]
    tm = GROUP * nseg
    lo = GROUP if grid2d else 0
    ext = tm + 2 * lo
    nlb = D_MODEL // LANES
    t = pl.program_id(0)
    top = (t % tpi) == 0
    bottom = (t % tpi) == tpi - 1

    yu_ref[lo:lo + tm, 0:SSD_WIDTH] = y_ref[...]
    yu_ref[lo:lo + tm, SSD_WIDTH:] = u_ref[...]
    xext_ref[lo:lo + tm, :] = x_ref[...]
    if grid2d:
        yu_ref[0:lo, 0:SSD_WIDTH] = yp_ref[...]
        yu_ref[0:lo, SSD_WIDTH:] = up_ref[...]
        xext_ref[0:lo, :] = xp_ref[...]
        yu_ref[lo + tm:, 0:SSD_WIDTH] = yn_ref[...]
        yu_ref[lo + tm:, SSD_WIDTH:] = un_ref[...]
        xext_ref[lo + tm:, :] = xn_ref[...]
    mix = _dot(yu_ref[...], wo_ref[...])
    ms = jnp.mean(mix * mix, axis=-1, keepdims=True)
    x1 = xext_ref[...] + g1_ref[...] * (mix * lax.rsqrt(ms + EPS) * gpost1_ref[...])
    x1_ref[...] = x1[lo:lo + tm, :]
    ms2 = jnp.mean(x1 * x1, axis=-1, keepdims=True)
    h = x1 * lax.rsqrt(ms2 + EPS) * (gpre2_ref[...] * (1.0 + sc_ref[...])) + sh_ref[...]

    for j in range(nlb):
        hs_ref[j] = h[:, j * LANES:(j + 1) * LANES]
    for j in range(nlb):
        for g in range(ext // GROUP):
            grp = jnp.concatenate(
                [hs_ref[j, pl.ds(g * GROUP + i, SUBLANES, stride=SUBLANES), :]
                 for i in range(SUBLANES)], axis=0)
            if grid2d and g == 0:
                grp = jnp.where(top, 0.0, grp)
            if grid2d and g == ext // GROUP - 1:
                grp = jnp.where(bottom, 0.0, grp)
            hb_ref[g * GROUP:(g + 1) * GROUP, j * LANES:(j + 1) * LANES] = grp.astype(BF16)

    sub = lax.broadcasted_iota(jnp.int32, (SUBLANES, 1), 0)

    def shift_down(v, fill):
        edge = 0.0 if fill is None else pltpu.roll(fill, 1, axis=0)
        return jnp.where(sub > 0, pltpu.roll(v, 1, axis=0), edge)

    def shift_up(v, fill):
        edge = 0.0 if fill is None else pltpu.roll(fill, SUBLANES - 1, axis=0)
        return jnp.where(sub < SUBLANES - 1, pltpu.roll(v, SUBLANES - 1, axis=0), edge)

    def project(k0, blk):
        blk[:, 0:MLP_KW] = _dot(hb_ref[...], wup_ref[:, pl.ds(k0, MLP_KW)])
        blk[:, MLP_KW:2 * MLP_KW] = _dot(hb_ref[...], wup_ref[:, pl.ds(D_FF + k0, MLP_KW)])

    def conv_vreg(blk, s, i, cols, w, bias, loaded):
        def column(g, k, dc):
            def vreg(gg):
                if (gg, k) not in loaded:
                    r0 = gg * GROUP + k * SUBLANES
                    loaded[(gg, k)] = blk[r0:r0 + SUBLANES, cols]
                return loaded[(gg, k)]
            if grid2d:
                acc = w[dc:dc + 1, :] * vreg(g)
                for dr in (1, 2):
                    acc = acc + w[3 * dr + dc:3 * dr + dc + 1, :] * vreg(g + dr)
                return acc
            return w[3 + dc:4 + dc, :] * vreg(g)

        if i > 0:
            left = column(s, i - 1, 0)
        else:
            before = column(s - 1, SUBLANES - 1, 0) if (not grid2d and s % tpi > 0) else None
            left = shift_down(column(s, SUBLANES - 1, 0), before)
        if i < SUBLANES - 1:
            right = column(s, i + 1, 2)
        else:
            after = column(s + 1, 0, 2) if (not grid2d and (s + 1) % tpi > 0) else None
            right = shift_up(column(s, 0, 2), after)
        return column(s, i, 1) + bias + left + right

    def gate_slice(k0, blk):
        for q in range(MLP_KW // LANES):
            lg = pl.multiple_of(k0 + q * LANES, LANES)
            lv = pl.multiple_of(D_FF + k0 + q * LANES, LANES)
            wg, wv = cw_ref[:, pl.ds(lg, LANES)], cw_ref[:, pl.ds(lv, LANES)]
            bg, bv = cb_ref[:, pl.ds(lg, LANES)], cb_ref[:, pl.ds(lv, LANES)]
            cg = slice(q * LANES, (q + 1) * LANES)
            cv = slice(MLP_KW + q * LANES, MLP_KW + (q + 1) * LANES)
            gate_in, val_in = {}, {}
            for s in range(nseg):
                for i0 in range(0, SUBLANES, 2):
                    act = jnp.concatenate(
                        [_silu(conv_vreg(blk, s, i, cg, wg, bg, gate_in))
                         * conv_vreg(blk, s, i, cv, wv, bv, val_in) for i in (i0, i0 + 1)],
                        axis=0)
                    r0 = s * GROUP + i0 * SUBLANES
                    act_ref[r0:r0 + 2 * SUBLANES, pl.ds(lg, LANES)] = act.astype(BF16)

    def slice_pair(jj, carry):
        k0 = pl.multiple_of(jj * (2 * MLP_KW), 2 * MLP_KW)
        project(k0 + MLP_KW, blk_b)
        gate_slice(k0, blk_a)
        project(k0 + 2 * MLP_KW, blk_a)
        gate_slice(k0 + MLP_KW, blk_b)
        return carry

    n_slices = D_FF // MLP_KW
    assert n_slices % 2 == 1 and n_slices * MLP_KW == D_FF
    project(0, blk_a)
    lax.fori_loop(0, (n_slices - 1) // 2, slice_pair, 0)
    gate_slice((n_slices - 1) * MLP_KW, blk_a)

    f = _dot(act_ref[...], wd_ref[...])
    msf = jnp.mean(f * f, axis=-1, keepdims=True)
    fn = f * lax.rsqrt(msf + EPS) * gpost2_ref[...]
    for j in range(nlb):
        for g in range(nseg):
            for i in range(SUBLANES):
                r0 = g * GROUP + i * SUBLANES
                fs_ref[j, pl.ds(g * GROUP + i, SUBLANES, stride=SUBLANES), :] = (
                    fn[r0:r0 + SUBLANES, j * LANES:(j + 1) * LANES])
    for j in range(nlb):
        lanes = slice(j * LANES, (j + 1) * LANES)
        o_ref[:, lanes] = x1_ref[:, lanes] + g2_ref[:, lanes] * fs_ref[j]


def _mlp(y, u, x2d, mod_row, grid2d, nseg, tpi, g1, sc, sh, g2, w):
    n = x2d.shape[0]
    seg = GROUP
    tm = seg * nseg
    ext = tm + 2 * seg if grid2d else tm
    nlb = D_MODEL // LANES
    n_seg_total = n // seg
    mod_spec = pl.BlockSpec((None, 1, D_MODEL), lambda i: (mod_row(i), 0, 0))
    tok = pl.BlockSpec((tm, D_MODEL), lambda i: (i, 0))
    prev_spec = pl.BlockSpec((seg, D_MODEL), lambda i: (jnp.maximum(i * nseg - 1, 0), 0))
    next_spec = pl.BlockSpec(
        (seg, D_MODEL), lambda i: (jnp.minimum((i + 1) * nseg, n_seg_total - 1), 0))
    halo_specs = [prev_spec] * 3 + [next_spec] * 3 if grid2d else []
    halo_args = [y, u, x2d, y, u, x2d] if grid2d else []
    weights = [w["wo"], w["gpost1"], w["gpre2"], w["wup"], w["ffn_cw"],
               w["ffn_cb"], w["wd"], w["gpost2"]]
    return pl.pallas_call(
        functools.partial(_mlp_kernel, grid2d, nseg, tpi),
        grid=(n // tm,),
        in_specs=[tok, tok, tok] + halo_specs + [mod_spec] * 4
                 + [_const_spec(a.shape) for a in weights],
        out_specs=tok,
        out_shape=jax.ShapeDtypeStruct((n, D_MODEL), F32),
        scratch_shapes=[pltpu.VMEM((ext, SSD_WIDTH + CF_WIDTH), BF16),
                        pltpu.VMEM((ext, D_MODEL), F32),
                        pltpu.VMEM((nlb, ext, LANES), F32),
                        pltpu.VMEM((ext, D_MODEL), BF16),
                        pltpu.VMEM((tm, D_MODEL), F32),
                        pltpu.VMEM((ext, 2 * MLP_KW), F32),
                        pltpu.VMEM((ext, 2 * MLP_KW), F32),
                        pltpu.VMEM((tm, D_FF), BF16),
                        pltpu.VMEM((nlb, tm, LANES), F32)],
        compiler_params=_params(1),
        name="mlp_grid" if grid2d else "mlp_seq",
    )(y, u, x2d, *halo_args, g1, sc, sh, g2, *weights)


def _state_to_kernel_layout(s):
    return s.reshape(s.shape[0], SSD_WIDTH, D_STATE)


def _state_from_kernel_layout(s):
    return s.reshape(s.shape[0], SSD_HEADS, SSD_HEAD_DIM, D_STATE)


def _trunk_layer(x, mod, mod_row_of_batch, w, init_f, init_b, rows):
    bsz, seqlen, _ = x.shape
    x2d = x.reshape(bsz * seqlen, D_MODEL)
    sh1, sc1, g1, sh2, sc2, g2 = mod
    latent = rows is not None
    tm = 512 if latent else seqlen
    tps = seqlen // tm
    mod_row = lambda i: mod_row_of_batch(i // tps)

    tm_in = min(IN_PROJ_TILE, seqlen)
    z, xbc, dtf, dtb, dtft, dtbt, u = _in_proj(
        x2d, seqlen, lambda i: mod_row_of_batch(i // (seqlen // tm_in)), tm_in, sc1, sh1,
        w["gpre1"], w)

    zero_init = init_f is None
    yb, fin_b = _ssd_sweep(False, zero_init, bsz, seqlen, xbc, dtb, dtbt, w["alog_b"],
                           init_b, None)
    y, fin_f = _ssd_sweep(True, zero_init, bsz, seqlen, xbc, dtf, dtft, w["alog_f"],
                          init_f, (yb, z, w["dskip"], w["ssd_norm"]))

    ucf = _cf_module(u, seqlen, min(CF_TILE, seqlen), w["cf_w"], w["cf_b"], w["cf_g"],
                     w["cf_beta"])
    nseg = MLP_TILE // GROUP
    if latent:
        assert GRID_W == GROUP and tm == MLP_TILE
        out = _mlp(y, ucf, x2d, mod_row, True, nseg, rows // nseg, g1, sc2, sh2, g2, w)
    else:
        assert MLP_TILE % seqlen == 0
        out = _mlp(y, ucf, x2d, mod_row, False, nseg, seqlen // GROUP, g1, sc2, sh2, g2, w)
    return out.reshape(bsz, seqlen, D_MODEL), fin_f, fin_b


def _layer_weights(l, w_in, w_ssd_conv, b_ssd_conv, a_log_fwd, a_log_bwd, dt_bias_fwd,
                   dt_bias_bwd, d_skip, ssd_norm, w_cf_conv, b_cf_conv, cf_ln_g, cf_ln_b,
                   w_out, norm_mix_pre, norm_mix_post, norm_ffn_pre, norm_ffn_post,
                   w_ffn_up, w_ffn_conv, b_ffn_conv, w_ffn_down):
    wi = w_in[l].astype(BF16)
    o = 0
    wz = wi[:, o:o + SSD_WIDTH]; o += SSD_WIDTH
    wx = wi[:, o:o + CONV_CH]; o += CONV_CH
    wdf = wi[:, o:o + SSD_HEADS]; o += SSD_HEADS
    wdb = wi[:, o:o + SSD_HEADS]; o += SSD_HEADS
    wa = wi[:, o:o + CF_WIDTH]; o += CF_WIDTH
    wg = wi[:, o:o + CF_WIDTH]
    pad_l = lambda a: jnp.pad(a, ((0, 0), (0, LANES - a.shape[1])))
    row = lambda a: a.reshape(1, -1)
    wo = w_out[l].astype(BF16)
    cf_w = jnp.pad(w_cf_conv[l], ((0, 32 - CF_KERNEL), (0, 0)))
    ffn_cw = jnp.pad(w_ffn_conv[l].reshape(9, 2 * D_FF), ((0, 7), (0, 0)))
    return {
        "wz": wz, "wx": wx, "wdt": jnp.concatenate([pad_l(wdf), pad_l(wdb)], axis=1),
        "wa": wa, "wg": wg,
        "bdt": jnp.concatenate([pad_l(row(dt_bias_fwd[l])), pad_l(row(dt_bias_bwd[l]))],
                               axis=1),
        "gpre1": row(norm_mix_pre[l]), "gpost1": row(norm_mix_post[l]),
        "gpre2": row(norm_ffn_pre[l]), "gpost2": row(norm_ffn_post[l]),
        "alog_f": a_log_fwd[l], "alog_b": a_log_bwd[l],
        "ssd_cw": jnp.pad(w_ssd_conv[l], ((0, 5), (0, 0))), "ssd_cb": row(b_ssd_conv[l]),
        "dskip": row(jnp.repeat(d_skip[l], SSD_HEAD_DIM)), "ssd_norm": row(ssd_norm[l]),
        "cf_w": cf_w, "cf_b": row(b_cf_conv[l]), "cf_g": row(cf_ln_g[l]),
        "cf_beta": row(cf_ln_b[l]),
        "wo": wo,
        "wup": w_ffn_up[l].astype(BF16), "ffn_cw": ffn_cw, "ffn_cb": row(b_ffn_conv[l]),
        "wd": w_ffn_down[l].astype(BF16),
    }


def kernel(x_prompt, x_sample, state_ssd_fwd, state_ssd_bwd, c, c_ctx, w_ada, b_ada, norm_mix_pre, norm_mix_post, w_in, w_ssd_conv, b_ssd_conv, a_log_fwd, a_log_bwd, dt_bias_fwd, dt_bias_bwd, d_skip, ssd_norm, w_cf_conv, b_cf_conv, cf_ln_g, cf_ln_b, w_out, norm_ffn_pre, norm_ffn_post, w_ffn_up, w_ffn_conv, b_ffn_conv, w_ffn_down):
    depth = w_ada.shape[0]
    dec_batch = x_sample.shape[0]
    rows = x_sample.shape[1] // GRID_W
    ctx_row = dec_batch
    n_cond = -(-(dec_batch + 1) // 8) * 8
    cond = jnp.zeros((n_cond, D_MODEL), F32)
    cond = cond.at[:dec_batch].set(c).at[ctx_row].set(c_ctx)

    xp, xl = x_prompt, x_sample
    new_f, new_b = [], []
    for l in range(depth):
        w = _layer_weights(l, w_in, w_ssd_conv, b_ssd_conv, a_log_fwd, a_log_bwd,
                           dt_bias_fwd, dt_bias_bwd, d_skip, ssd_norm, w_cf_conv, b_cf_conv,
                           cf_ln_g, cf_ln_b, w_out, norm_mix_pre, norm_mix_post,
                           norm_ffn_pre, norm_ffn_post, w_ffn_up, w_ffn_conv, b_ffn_conv,
                           w_ffn_down)
        mod = _modulation(cond, w_ada[l], b_ada[l])
        mod = [m.reshape(n_cond, 1, D_MODEL) for m in jnp.split(mod, 6, axis=-1)]
        xp, s_f, s_b = _trunk_layer(xp, mod, lambda b: ctx_row, w, None, None, None)
        new_f.append(_state_from_kernel_layout(s_f))
        new_b.append(_state_from_kernel_layout(s_b))
        xl, _, _ = _trunk_layer(xl, mod, lambda b: b, w,
                                _state_to_kernel_layout(state_ssd_fwd[:, l]),
                                _state_to_kernel_layout(state_ssd_bwd[:, l]), rows)
    return (xp, xl, jnp.stack(new_f, axis=1), jnp.stack(new_b, axis=1))
```

```python
import functools

import jax
import jax.numpy as jnp
from jax import lax
from jax.experimental import pallas as pl
from jax.experimental.pallas import tpu as pltpu

D_MODEL = 1024
GRID_W = 64
SSD_WIDTH = 1024
SSD_HEAD_DIM = 64
SSD_HEADS = 16
N_GROUPS = 2
D_STATE = 128
CHUNK = 128
CONV_CH = SSD_WIDTH + 2 * N_GROUPS * D_STATE
CF_WIDTH = 1024
CF_KERNEL = 31
D_FF = 2816
EPS = 1e-6

LANES = 128
SUBLANES = 8
HALO = 16
VMEM_LIMIT = 56 * 1024 * 1024

F32 = jnp.float32
BF16 = jnp.bfloat16
HIGHEST = lax.Precision.HIGHEST


def _params(n_axes):
    return pltpu.CompilerParams(
        dimension_semantics=("arbitrary",) * n_axes, vmem_limit_bytes=VMEM_LIMIT)


def _const_spec(shape):
    nd = len(shape)
    return pl.BlockSpec(shape, lambda *_: (0,) * nd, pipeline_mode=pl.Buffered(1))


def _silu(v):
    return v * jax.nn.sigmoid(v)


def _softplus(v):
    return jnp.maximum(v, 0.0) + jnp.log1p(jnp.exp(-jnp.abs(v)))


def _dot(a, b):
    return jnp.dot(a, b, preferred_element_type=F32)


def _mod_kernel(c_ref, w_ref, b_ref, o_ref):
    s = _silu(c_ref[...])
    o_ref[...] = jnp.dot(s, w_ref[...], precision=HIGHEST,
                         preferred_element_type=F32) + b_ref[...]


def _modulation(cond, w_ada, b_ada):
    rows = cond.shape[0]
    n = w_ada.shape[1]
    tn = 1024
    return pl.pallas_call(
        _mod_kernel,
        grid=(n // tn,),
        in_specs=[_const_spec((rows, D_MODEL)),
                  pl.BlockSpec((D_MODEL, tn), lambda j: (0, j)),
                  pl.BlockSpec((1, tn), lambda j: (0, j))],
        out_specs=pl.BlockSpec((rows, tn), lambda j: (0, j)),
        out_shape=jax.ShapeDtypeStruct((rows, n), F32),
        compiler_params=_params(1),
        name="modulation",
    )(cond, w_ada, b_ada.reshape(1, n))


XBC_SPLIT = 3
IN_PROJ_TILE = 1024


def _in_proj_kernel(tm, tps, x_ref, xp_ref, xn_ref, sc_ref, sh_ref, g_ref,
                    wz_ref, wx_ref, wdt_ref, wa_ref, wg_ref, bdt_ref, cw_ref, cb_ref,
                    z_ref, xbc_ref, dtf_ref, dtb_ref, dtft_ref, dtbt_ref, u_ref,
                    hb_ref, *ext_refs):
    t = pl.program_id(0)
    first = (t % tps) == 0
    last = (t % tps) == tps - 1
    scale = g_ref[...] * (1.0 + sc_ref[...])
    shift = sh_ref[...]

    def mod_norm(x):
        ms = jnp.mean(x * x, axis=-1, keepdims=True)
        return (x * lax.rsqrt(ms + EPS) * scale + shift).astype(BF16)

    zero = jnp.zeros((HALO, D_MODEL), BF16)
    hb_ref[0:HALO, :] = jnp.where(first, zero, mod_norm(xp_ref[...]))
    hb_ref[HALO:HALO + tm, :] = mod_norm(x_ref[...])
    hb_ref[HALO + tm:2 * HALO + tm, :] = jnp.where(last, zero, mod_norm(xn_ref[...]))
    hb = hb_ref[HALO:HALO + tm, :]

    piece = CONV_CH // XBC_SPLIT

    def project(p):
        ext_refs[p][...] = _dot(hb_ref[...], wx_ref[:, p * piece:(p + 1) * piece])

    def conv(p):
        cols = slice(p * piece, (p + 1) * piece)
        cw = cw_ref[:, cols]
        ext_ref = ext_refs[p]
        xc = (cw[1:2, :] * ext_ref[HALO:HALO + tm, :] + cb_ref[:, cols]
              + cw[0:1, :] * ext_ref[HALO - 1:HALO - 1 + tm, :]
              + cw[2:3, :] * ext_ref[HALO + 1:HALO + 1 + tm, :])
        xbc_ref[:, cols] = _silu(xc).astype(BF16)

    project(0)
    for p in range(1, XBC_SPLIT):
        project(p)
        conv(p - 1)
    z_ref[...] = _dot(hb, wz_ref[...]).astype(BF16)
    conv(XBC_SPLIT - 1)

    a = _dot(hb, wa_ref[...])
    g = _dot(hb, wg_ref[...])
    u_ref[...] = (a * jax.nn.sigmoid(g)).astype(BF16)
    dt = _softplus(_dot(hb, wdt_ref[...]) + bdt_ref[...])
    dtf, dtb = dt[:, :LANES], dt[:, LANES:]
    dtf_ref[...] = dtf
    dtb_ref[...] = dtb
    dtft_ref[...] = dtf.T[:SSD_HEADS, :]
    dtbt_ref[...] = dtb.T[:SSD_HEADS, :]


def _in_proj(x2d, seqlen, mod_row, tm, sc, sh, g, w):
    n = x2d.shape[0]
    hb = tm // HALO
    n_halo = n // HALO
    mod_spec = pl.BlockSpec((None, 1, D_MODEL), lambda i: (mod_row(i), 0, 0))
    tok = lambda width: pl.BlockSpec((tm, width), lambda i: (i, 0))
    tok_t = pl.BlockSpec((SSD_HEADS, tm), lambda i: (0, i))
    prev_spec = pl.BlockSpec((HALO, D_MODEL), lambda i: (jnp.maximum(i * hb - 1, 0), 0))
    next_spec = pl.BlockSpec((HALO, D_MODEL),
                             lambda i: (jnp.minimum((i + 1) * hb, n_halo - 1), 0))
    weights = [w["wz"], w["wx"], w["wdt"], w["wa"], w["wg"], w["bdt"], w["ssd_cw"], w["ssd_cb"]]
    return pl.pallas_call(
        functools.partial(_in_proj_kernel, tm, seqlen // tm),
        grid=(n // tm,),
        in_specs=[tok(D_MODEL), prev_spec, next_spec, mod_spec, mod_spec,
                  _const_spec((1, D_MODEL))]
                 + [_const_spec(a.shape) for a in weights],
        scratch_shapes=[pltpu.VMEM((tm + 2 * HALO, D_MODEL), BF16)]
                       + [pltpu.VMEM((tm + 2 * HALO, CONV_CH // XBC_SPLIT), F32)] * XBC_SPLIT,
        out_specs=[tok(SSD_WIDTH), tok(CONV_CH), tok(LANES), tok(LANES), tok_t, tok_t,
                   tok(CF_WIDTH)],
        out_shape=[jax.ShapeDtypeStruct((n, SSD_WIDTH), BF16),
                   jax.ShapeDtypeStruct((n, CONV_CH), BF16),
                   jax.ShapeDtypeStruct((n, LANES), F32),
                   jax.ShapeDtypeStruct((n, LANES), F32),
                   jax.ShapeDtypeStruct((SSD_HEADS, n), F32),
                   jax.ShapeDtypeStruct((SSD_HEADS, n), F32),
                   jax.ShapeDtypeStruct((n, CF_WIDTH), BF16)],
        compiler_params=_params(1),
        name="in_proj",
    )(x2d, x2d, x2d, sc, sh, g, *weights)


LOG2E = 1.4426950408889634
N_PAIRS = SSD_HEADS // 2
SSD_CHUNKS_PER_STEP = 4


def _ssd_kernel(fwd, zero_init, cps, *refs):
    refs = list(refs)
    xbc_ref, dt_ref, dtt_ref, alr_ref, alc_ref = refs[:5]
    del refs[:5]
    init_ref = None if zero_init else refs.pop(0)
    if fwd:
        yb_ref, z_ref, dskip_ref, nrm_ref = refs[:4]
        del refs[:4]
    y_ref, fin_ref = refs[:2]
    s_refs = refs[2:2 + N_PAIRS]
    yacc_refs = refs[2 + N_PAIRS:]
    j = pl.program_id(1)

    @pl.when(j == 0)
    def _():
        for pair in range(N_PAIRS):
            if zero_init:
                s_refs[pair][...] = jnp.zeros((D_STATE, LANES), F32)
            else:
                s_refs[pair][...] = init_ref[pair * LANES:(pair + 1) * LANES, :].T

    ri = lax.broadcasted_iota(jnp.int32, (CHUNK, CHUNK), 0)
    ci = lax.broadcasted_iota(jnp.int32, (CHUNK, CHUNK), 1)
    keep = (ri >= ci) if fwd else (ri <= ci)
    tri = keep.astype(F32)
    lane = lax.broadcasted_iota(jnp.int32, (1, LANES), 1)
    a_row = jnp.where(lane < SSD_HEADS, -jnp.exp(alr_ref[...]) * LOG2E, 0.0)
    a_col = -jnp.exp(alc_ref[...]) * LOG2E
    half = lane < SSD_HEAD_DIM
    nt = (((1,), (1,)), ((), ()))
    heads_per_group = SSD_HEADS // N_GROUPS
    end = CHUNK - 1 if fwd else 0

    pre = []
    for sc in range(cps):
        rows = slice(sc * CHUNK, (sc + 1) * CHUNK)
        dt_row = dtt_ref[:, rows]
        cum_col = jnp.dot(tri, dt_ref[rows, :] * a_row, precision=HIGHEST,
                          preferred_element_type=F32)
        cum_row = lax.dot_general(dt_row * a_col, tri, nt, precision=HIGHEST,
                                  preferred_element_type=F32)
        cum_end = cum_row[:, end:end + 1]
        p = {
            "cum_col": cum_col,
            "wgt_row": jnp.exp2(cum_end - cum_row) * dt_row,
            "edec": jnp.exp2(cum_end),
            "ecol": jnp.exp2(cum_col),
            "src_row": cum_row - jnp.log2(dt_row),
            "cb": [], "cg": [], "bgt": [],
        }
        for grp in range(N_GROUPS):
            b_cols = slice(SSD_WIDTH + grp * D_STATE, SSD_WIDTH + (grp + 1) * D_STATE)
            c_cols = slice(SSD_WIDTH + (N_GROUPS + grp) * D_STATE,
                           SSD_WIDTH + (N_GROUPS + grp + 1) * D_STATE)
            p["cb"].append(lax.dot_general(xbc_ref[rows, c_cols], xbc_ref[rows, b_cols], nt,
                                           preferred_element_type=F32))
            p["cg"].append(xbc_ref[rows, c_cols].astype(F32))
            p["bgt"].append(xbc_ref[rows, b_cols].astype(F32).T)
        pre.append(p)

    for sc in (range(cps) if fwd else reversed(range(cps))):
        rows = slice(sc * CHUNK, (sc + 1) * CHUNK)
        p = pre[sc]
        cum_col, wgt_row, edec, ecol, src_row = (
            p["cum_col"], p["wgt_row"], p["edec"], p["ecol"], p["src_row"])
        for grp in range(N_GROUPS):
            cb, cg, bgt = p["cb"][grp], p["cg"][grp], p["bgt"][grp]
            for pair in range(grp * heads_per_group // 2, (grp + 1) * heads_per_group // 2):
                lanes = slice(pair * LANES, (pair + 1) * LANES)
                xp = xbc_ref[rows, lanes]
                zero = jnp.zeros_like(xp)
                x_bd = jnp.concatenate(
                    [jnp.where(half, xp, zero), jnp.where(half, zero, xp)], axis=0)
                sp = s_refs[pair][...]
                spb = sp.astype(BF16)
                s_bd = jnp.concatenate(
                    [jnp.where(half, spb, zero), jnp.where(half, zero, spb)], axis=0)
                m_parts, c_parts, b_parts = [], [], []
                for h in (2 * pair, 2 * pair + 1):
                    seg = jnp.where(keep, cum_col[:, h:h + 1] - src_row[h:h + 1, :], -1e30)
                    m_parts.append((cb * jnp.exp2(seg)).astype(BF16))
                    c_parts.append((cg * ecol[:, h:h + 1]).astype(BF16))
                    b_parts.append((bgt * wgt_row[h:h + 1, :]).astype(BF16))
                lhs = jnp.concatenate(m_parts + c_parts, axis=1)
                rhs = jnp.concatenate([x_bd, s_bd], axis=0)
                y_pair = _dot(lhs, rhs)
                if fwd:
                    yacc_refs[sc * N_PAIRS + pair][...] = (
                        y_pair + xp.astype(F32) * dskip_ref[:, lanes])
                else:
                    y_ref[rows, lanes] = y_pair.astype(BF16)
                dec = jnp.where(half, edec[2 * pair:2 * pair + 1, :],
                                edec[2 * pair + 1:2 * pair + 2, :])
                s_refs[pair][...] = sp * dec + _dot(jnp.concatenate(b_parts, axis=1), x_bd)

        if fwd:
            yz = []
            for pair in range(N_PAIRS):
                lanes = slice(pair * LANES, (pair + 1) * LANES)
                y = yacc_refs[sc * N_PAIRS + pair][...] + yb_ref[rows, lanes].astype(F32)
                yz.append(y * _silu(z_ref[rows, lanes].astype(F32)))
            sq = yz[0] * yz[0]
            for v in yz[1:]:
                sq = sq + v * v
            rstd = lax.rsqrt(jnp.sum(sq, axis=-1, keepdims=True) * (1.0 / SSD_WIDTH) + EPS)
            for pair in range(N_PAIRS):
                lanes = slice(pair * LANES, (pair + 1) * LANES)
                y_ref[rows, lanes] = (yz[pair] * rstd * nrm_ref[:, lanes]).astype(BF16)

    @pl.when(j == pl.num_programs(1) - 1)
    def _():
        for pair in range(N_PAIRS):
            fin_ref[pair * LANES:(pair + 1) * LANES, :] = s_refs[pair][...].T


def _ssd_sweep(fwd, zero_init, bsz, seqlen, xbc, dt, dtt, alog, init, extra):
    cps = min(SSD_CHUNKS_PER_STEP, seqlen // CHUNK)
    blk = cps * CHUNK
    ns = seqlen // blk

    def block_of(j):
        return j if fwd else ns - 1 - j

    tok = lambda width: pl.BlockSpec(
        (blk, width), lambda b, j: (b * ns + block_of(j), 0))
    dtt_spec = pl.BlockSpec((SSD_HEADS, blk), lambda b, j: (0, b * ns + block_of(j)))
    state_spec = pl.BlockSpec((None, SSD_WIDTH, D_STATE), lambda b, j: (b, 0, 0))
    alog_row = jnp.pad(alog.reshape(1, SSD_HEADS), ((0, 0), (0, LANES - SSD_HEADS)))
    alog_col = alog.reshape(SSD_HEADS, 1)
    in_specs = [tok(CONV_CH), tok(LANES), dtt_spec,
                _const_spec((1, LANES)), _const_spec((SSD_HEADS, 1))]
    args = [xbc, dt, dtt, alog_row, alog_col]
    if not zero_init:
        in_specs.append(state_spec)
        args.append(init)
    if fwd:
        yb, z, dskip, nrm = extra
        in_specs += [tok(SSD_WIDTH), tok(SSD_WIDTH), _const_spec((1, SSD_WIDTH)),
                     _const_spec((1, SSD_WIDTH))]
        args += [yb, z, dskip, nrm]
    return pl.pallas_call(
        functools.partial(_ssd_kernel, fwd, zero_init, cps),
        grid=(bsz, ns),
        in_specs=in_specs,
        out_specs=[tok(SSD_WIDTH), state_spec],
        out_shape=[jax.ShapeDtypeStruct((bsz * seqlen, SSD_WIDTH), BF16),
                   jax.ShapeDtypeStruct((bsz, SSD_WIDTH, D_STATE), F32)],
        scratch_shapes=[pltpu.VMEM((D_STATE, LANES), F32)] * N_PAIRS
                       + [pltpu.VMEM((CHUNK, LANES), F32)] * (cps * N_PAIRS if fwd else 0),
        compiler_params=_params(2),
        name="ssd_fwd" if fwd else "ssd_bwd",
    )(*args)


def _cf_pitches(tm):
    seg = tm // SUBLANES
    in_pitch = -(-(seg + CF_KERNEL - 1 - 4) // 8) * 8 + 4
    out_pitch = seg + 8
    return seg, in_pitch, out_pitch


def _cf_kernel(tm, tps, u_ref, up_ref, un_ref, w_ref, b_ref, g_ref, beta_ref, o_ref,
               buf_ref, slab_ref, acc_ref):
    seg, in_pitch, out_pitch = _cf_pitches(tm)
    pad = (CF_KERNEL - 1) // 2
    t = pl.program_id(0)
    first = (t % tps) == 0
    last = (t % tps) == tps - 1
    buf_ref[0:HALO, :] = jnp.where(first, 0.0, up_ref[...].astype(F32))
    buf_ref[HALO:HALO + tm, :] = u_ref[...].astype(F32)
    buf_ref[HALO + tm:2 * HALO + tm, :] = jnp.where(last, 0.0, un_ref[...].astype(F32))
    span = seg + 2 * pad
    for jb in range(CF_WIDTH // LANES):
        for r in range(SUBLANES):
            src = HALO - pad + r * seg
            slab_ref[jb, r * in_pitch:r * in_pitch + span, :] = (
                buf_ref[src:src + span, jb * LANES:(jb + 1) * LANES])

    def lane_block(jb, carry):
        l0 = pl.multiple_of(jb * LANES, LANES)
        w = w_ref[:, pl.ds(l0, LANES)]
        taps = [jnp.broadcast_to(w[k:k + 1, :], (SUBLANES, LANES)) for k in range(CF_KERNEL)]
        bias = jnp.broadcast_to(b_ref[:, pl.ds(l0, LANES)], (SUBLANES, LANES))
        for i in range(seg):
            acc = bias
            for k in range(CF_KERNEL):
                acc = acc + taps[k] * slab_ref[jb, pl.ds(i + k, SUBLANES, stride=in_pitch), :]
            acc_ref[jb, pl.ds(i, SUBLANES, stride=out_pitch), :] = acc
        return carry

    lax.fori_loop(0, CF_WIDTH // LANES, lane_block, 0)

    nb = CF_WIDTH // LANES
    for r in range(SUBLANES):
        v = [acc_ref[jb, r * out_pitch:r * out_pitch + seg, :] for jb in range(nb)]
        mu = jnp.sum(sum(v[1:], v[0]), axis=-1, keepdims=True) * (1.0 / CF_WIDTH)
        d = [vj - mu for vj in v]
        sq = d[0] * d[0]
        for dj in d[1:]:
            sq = sq + dj * dj
        rstd = lax.rsqrt(jnp.sum(sq, axis=-1, keepdims=True) * (1.0 / CF_WIDTH) + EPS)
        for jb in range(nb):
            lanes = slice(jb * LANES, (jb + 1) * LANES)
            y = d[jb] * rstd * g_ref[:, lanes] + beta_ref[:, lanes]
            o_ref[r * seg:(r + 1) * seg, lanes] = _silu(y).astype(BF16)


def _cf_module(u, seqlen, tm, w, b, g, beta):
    n = u.shape[0]
    tps = seqlen // tm
    hb = tm // HALO
    n_halo = n // HALO
    _, in_pitch, out_pitch = _cf_pitches(tm)
    nb = CF_WIDTH // LANES
    return pl.pallas_call(
        functools.partial(_cf_kernel, tm, tps),
        grid=(n // tm,),
        in_specs=[pl.BlockSpec((tm, CF_WIDTH), lambda t: (t, 0)),
                  pl.BlockSpec((HALO, CF_WIDTH), lambda t: (jnp.maximum(t * hb - 1, 0), 0)),
                  pl.BlockSpec((HALO, CF_WIDTH),
                               lambda t: (jnp.minimum((t + 1) * hb, n_halo - 1), 0)),
                  _const_spec(w.shape), _const_spec(b.shape), _const_spec(g.shape),
                  _const_spec(beta.shape)],
        out_specs=pl.BlockSpec((tm, CF_WIDTH), lambda t: (t, 0)),
        out_shape=jax.ShapeDtypeStruct((n, CF_WIDTH), BF16),
        scratch_shapes=[pltpu.VMEM((tm + 2 * HALO, CF_WIDTH), F32),
                        pltpu.VMEM((nb, SUBLANES * in_pitch, LANES), F32),
                        pltpu.VMEM((nb, SUBLANES * out_pitch, LANES), F32)],
        compiler_params=_params(1),
        name="cf_module",
    )(u, u, u, w, b, g, beta)


MLP_KW = 256
MLP_TILE = 512
CF_TILE = 1024
GROUP = SUBLANES * SUBLANES


def _mlp_kernel(grid2d, nseg, tpi, *refs):
    refs = list(refs)
    y_ref, u_ref, x_ref = refs[:3]
    del refs[:3]
    if grid2d:
        yp_ref, up_ref, xp_ref, yn_ref, un_ref, xn_ref = refs[:6]
        del refs[:6]
    (g1_ref, sc_ref, sh_ref, g2_ref, wo_ref, gpost1_ref, gpre2_ref, wup_ref,
     cw_ref, cb_ref, wd_ref, gpost2_ref, o_ref) = refs[:13]
    yu_ref, xext_ref, hs_ref, hb_ref, x1_ref, blk_a, blk_b, act_ref, fs_ref = refs[13:]
    tm = GROUP * nseg
    lo = GROUP if grid2d else 0
    ext = tm + 2 * lo
    nlb = D_MODEL // LANES
    t = pl.program_id(0)
    top = (t % tpi) == 0
    bottom = (t % tpi) == tpi - 1

    yu_ref[lo:lo + tm, 0:SSD_WIDTH] = y_ref[...]
    yu_ref[lo:lo + tm, SSD_WIDTH:] = u_ref[...]
    xext_ref[lo:lo + tm, :] = x_ref[...]
    if grid2d:
        yu_ref[0:lo, 0:SSD_WIDTH] = yp_ref[...]
        yu_ref[0:lo, SSD_WIDTH:] = up_ref[...]
        xext_ref[0:lo, :] = xp_ref[...]
        yu_ref[lo + tm:, 0:SSD_WIDTH] = yn_ref[...]
        yu_ref[lo + tm:, SSD_WIDTH:] = un_ref[...]
        xext_ref[lo + tm:, :] = xn_ref[...]
    mix = _dot(yu_ref[...], wo_ref[...])
    ms = jnp.mean(mix * mix, axis=-1, keepdims=True)
    x1 = xext_ref[...] + g1_ref[...] * (mix * lax.rsqrt(ms + EPS) * gpost1_ref[...])
    x1_ref[...] = x1[lo:lo + tm, :]
    ms2 = jnp.mean(x1 * x1, axis=-1, keepdims=True)
    h = x1 * lax.rsqrt(ms2 + EPS) * (gpre2_ref[...] * (1.0 + sc_ref[...])) + sh_ref[...]

    for j in range(nlb):
        hs_ref[j] = h[:, j * LANES:(j + 1) * LANES]
    for j in range(nlb):
        for g in range(ext // GROUP):
            grp = jnp.concatenate(
                [hs_ref[j, pl.ds(g * GROUP + i, SUBLANES, stride=SUBLANES), :]
                 for i in range(SUBLANES)], axis=0)
            if grid2d and g == 0:
                grp = jnp.where(top, 0.0, grp)
            if grid2d and g == ext // GROUP - 1:
                grp = jnp.where(bottom, 0.0, grp)
            hb_ref[g * GROUP:(g + 1) * GROUP, j * LANES:(j + 1) * LANES] = grp.astype(BF16)

    sub = lax.broadcasted_iota(jnp.int32, (SUBLANES, 1), 0)

    def shift_down(v, fill):
        edge = 0.0 if fill is None else pltpu.roll(fill, 1, axis=0)
        return jnp.where(sub > 0, pltpu.roll(v, 1, axis=0), edge)

    def shift_up(v, fill):
        edge = 0.0 if fill is None else pltpu.roll(fill, SUBLANES - 1, axis=0)
        return jnp.where(sub < SUBLANES - 1, pltpu.roll(v, SUBLANES - 1, axis=0), edge)

    def project(k0, blk):
        blk[:, 0:MLP_KW] = _dot(hb_ref[...], wup_ref[:, pl.ds(k0, MLP_KW)])
        blk[:, MLP_KW:2 * MLP_KW] = _dot(hb_ref[...], wup_ref[:, pl.ds(D_FF + k0, MLP_KW)])

    def conv_vreg(blk, s, i, cols, w, bias, loaded):
        def column(g, k, dc):
            def vreg(gg):
                if (gg, k) not in loaded:
                    r0 = gg * GROUP + k * SUBLANES
                    loaded[(gg, k)] = blk[r0:r0 + SUBLANES, cols]
                return loaded[(gg, k)]
            if grid2d:
                acc = w[dc:dc + 1, :] * vreg(g)
                for dr in (1, 2):
                    acc = acc + w[3 * dr + dc:3 * dr + dc + 1, :] * vreg(g + dr)
                return acc
            return w[3 + dc:4 + dc, :] * vreg(g)

        if i > 0:
            left = column(s, i - 1, 0)
        else:
            before = column(s - 1, SUBLANES - 1, 0) if (not grid2d and s % tpi > 0) else None
            left = shift_down(column(s, SUBLANES - 1, 0), before)
        if i < SUBLANES - 1:
            right = column(s, i + 1, 2)
        else:
            after = column(s + 1, 0, 2) if (not grid2d and (s + 1) % tpi > 0) else None
            right = shift_up(column(s, 0, 2), after)
        return column(s, i, 1) + bias + left + right

    def gate_slice(k0, blk):
        for q in range(MLP_KW // LANES):
            lg = pl.multiple_of(k0 + q * LANES, LANES)
            lv = pl.multiple_of(D_FF + k0 + q * LANES, LANES)
            wg, wv = cw_ref[:, pl.ds(lg, LANES)], cw_ref[:, pl.ds(lv, LANES)]
            bg, bv = cb_ref[:, pl.ds(lg, LANES)], cb_ref[:, pl.ds(lv, LANES)]
            cg = slice(q * LANES, (q + 1) * LANES)
            cv = slice(MLP_KW + q * LANES, MLP_KW + (q + 1) * LANES)
            gate_in, val_in = {}, {}
            for s in range(nseg):
                for i0 in range(0, SUBLANES, 2):
                    act = jnp.concatenate(
                        [_silu(conv_vreg(blk, s, i, cg, wg, bg, gate_in))
                         * conv_vreg(blk, s, i, cv, wv, bv, val_in) for i in (i0, i0 + 1)],
                        axis=0)
                    r0 = s * GROUP + i0 * SUBLANES
                    act_ref[r0:r0 + 2 * SUBLANES, pl.ds(lg, LANES)] = act.astype(BF16)

    def slice_pair(jj, carry):
        k0 = pl.multiple_of(jj * (2 * MLP_KW), 2 * MLP_KW)
        project(k0 + MLP_KW, blk_b)
        gate_slice(k0, blk_a)
        project(k0 + 2 * MLP_KW, blk_a)
        gate_slice(k0 + MLP_KW, blk_b)
        return carry

    n_slices = D_FF // MLP_KW
    assert n_slices % 2 == 1 and n_slices * MLP_KW == D_FF
    project(0, blk_a)
    lax.fori_loop(0, (n_slices - 1) // 2, slice_pair, 0)
    gate_slice((n_slices - 1) * MLP_KW, blk_a)

    f = _dot(act_ref[...], wd_ref[...])
    msf = jnp.mean(f * f, axis=-1, keepdims=True)
    fn = f * lax.rsqrt(msf + EPS) * gpost2_ref[...]
    for j in range(nlb):
        for g in range(nseg):
            for i in range(SUBLANES):
                r0 = g * GROUP + i * SUBLANES
                fs_ref[j, pl.ds(g * GROUP + i, SUBLANES, stride=SUBLANES), :] = (
                    fn[r0:r0 + SUBLANES, j * LANES:(j + 1) * LANES])
    for j in range(nlb):
        lanes = slice(j * LANES, (j + 1) * LANES)
        o_ref[:, lanes] = x1_ref[:, lanes] + g2_ref[:, lanes] * fs_ref[j]


def _mlp(y, u, x2d, mod_row, grid2d, nseg, tpi, g1, sc, sh, g2, w):
    n = x2d.shape[0]
    seg = GROUP
    tm = seg * nseg
    ext = tm + 2 * seg if grid2d else tm
    nlb = D_MODEL // LANES
    n_seg_total = n // seg
    mod_spec = pl.BlockSpec((None, 1, D_MODEL), lambda i: (mod_row(i), 0, 0))
    tok = pl.BlockSpec((tm, D_MODEL), lambda i: (i, 0))
    prev_spec = pl.BlockSpec((seg, D_MODEL), lambda i: (jnp.maximum(i * nseg - 1, 0), 0))
    next_spec = pl.BlockSpec(
        (seg, D_MODEL), lambda i: (jnp.minimum((i + 1) * nseg, n_seg_total - 1), 0))
    halo_specs = [prev_spec] * 3 + [next_spec] * 3 if grid2d else []
    halo_args = [y, u, x2d, y, u, x2d] if grid2d else []
    weights = [w["wo"], w["gpost1"], w["gpre2"], w["wup"], w["ffn_cw"],
               w["ffn_cb"], w["wd"], w["gpost2"]]
    return pl.pallas_call(
        functools.partial(_mlp_kernel, grid2d, nseg, tpi),
        grid=(n // tm,),
        in_specs=[tok, tok, tok] + halo_specs + [mod_spec] * 4
                 + [_const_spec(a.shape) for a in weights],
        out_specs=tok,
        out_shape=jax.ShapeDtypeStruct((n, D_MODEL), F32),
        scratch_shapes=[pltpu.VMEM((ext, SSD_WIDTH + CF_WIDTH), BF16),
                        pltpu.VMEM((ext, D_MODEL), F32),
                        pltpu.VMEM((nlb, ext, LANES), F32),
                        pltpu.VMEM((ext, D_MODEL), BF16),
                        pltpu.VMEM((tm, D_MODEL), F32),
                        pltpu.VMEM((ext, 2 * MLP_KW), F32),
                        pltpu.VMEM((ext, 2 * MLP_KW), F32),
                        pltpu.VMEM((tm, D_FF), BF16),
                        pltpu.VMEM((nlb, tm, LANES), F32)],
        compiler_params=_params(1),
        name="mlp_grid" if grid2d else "mlp_seq",
    )(y, u, x2d, *halo_args, g1, sc, sh, g2, *weights)


def _state_to_kernel_layout(s):
    return s.reshape(s.shape[0], SSD_WIDTH, D_STATE)


def _state_from_kernel_layout(s):
    return s.reshape(s.shape[0], SSD_HEADS, SSD_HEAD_DIM, D_STATE)


def _trunk_layer(x, mod, mod_row_of_batch, w, init_f, init_b, rows):
    bsz, seqlen, _ = x.shape
    x2d = x.reshape(bsz * seqlen, D_MODEL)
    sh1, sc1, g1, sh2, sc2, g2 = mod
    latent = rows is not None
    tm = 512 if latent else seqlen
    tps = seqlen // tm
    mod_row = lambda i: mod_row_of_batch(i // tps)

    tm_in = min(IN_PROJ_TILE, seqlen)
    z, xbc, dtf, dtb, dtft, dtbt, u = _in_proj(
        x2d, seqlen, lambda i: mod_row_of_batch(i // (seqlen // tm_in)), tm_in, sc1, sh1,
        w["gpre1"], w)

    zero_init = init_f is None
    yb, fin_b = _ssd_sweep(False, zero_init, bsz, seqlen, xbc, dtb, dtbt, w["alog_b"],
                           init_b, None)
    y, fin_f = _ssd_sweep(True, zero_init, bsz, seqlen, xbc, dtf, dtft, w["alog_f"],
                          init_f, (yb, z, w["dskip"], w["ssd_norm"]))

    ucf = _cf_module(u, seqlen, min(CF_TILE, seqlen), w["cf_w"], w["cf_b"], w["cf_g"],
                     w["cf_beta"])
    nseg = MLP_TILE // GROUP
    if latent:
        assert GRID_W == GROUP and tm == MLP_TILE
        out = _mlp(y, ucf, x2d, mod_row, True, nseg, rows // nseg, g1, sc2, sh2, g2, w)
    else:
        assert MLP_TILE % seqlen == 0
        out = _mlp(y, ucf, x2d, mod_row, False, nseg, seqlen // GROUP, g1, sc2, sh2, g2, w)
    return out.reshape(bsz, seqlen, D_MODEL), fin_f, fin_b


def _layer_weights(l, w_in, w_ssd_conv, b_ssd_conv, a_log_fwd, a_log_bwd, dt_bias_fwd,
                   dt_bias_bwd, d_skip, ssd_norm, w_cf_conv, b_cf_conv, cf_ln_g, cf_ln_b,
                   w_out, norm_mix_pre, norm_mix_post, norm_ffn_pre, norm_ffn_post,
                   w_ffn_up, w_ffn_conv, b_ffn_conv, w_ffn_down):
    wi = w_in[l]
    o = 0
    wz = wi[:, o:o + SSD_WIDTH].astype(BF16); o += SSD_WIDTH
    wx = wi[:, o:o + CONV_CH].astype(BF16); o += CONV_CH
    wdf = wi[:, o:o + SSD_HEADS].astype(BF16); o += SSD_HEADS
    wdb = wi[:, o:o + SSD_HEADS].astype(BF16); o += SSD_HEADS
    wa = wi[:, o:o + CF_WIDTH].astype(BF16); o += CF_WIDTH
    wg = wi[:, o:o + CF_WIDTH].astype(BF16)
    pad_l = lambda a: jnp.pad(a, ((0, 0), (0, LANES - a.shape[1])))
    row = lambda a: a.reshape(1, -1)
    wo = w_out[l].astype(BF16)
    cf_w = jnp.pad(w_cf_conv[l], ((0, 32 - CF_KERNEL), (0, 0)))
    ffn_cw = jnp.pad(w_ffn_conv[l].reshape(9, 2 * D_FF), ((0, 7), (0, 0)))
    return {
        "wz": wz, "wx": wx, "wdt": jnp.concatenate([pad_l(wdf), pad_l(wdb)], axis=1),
        "wa": wa, "wg": wg,
        "bdt": jnp.concatenate([pad_l(row(dt_bias_fwd[l])), pad_l(row(dt_bias_bwd[l]))],
                               axis=1),
        "gpre1": row(norm_mix_pre[l]), "gpost1": row(norm_mix_post[l]),
        "gpre2": row(norm_ffn_pre[l]), "gpost2": row(norm_ffn_post[l]),
        "alog_f": a_log_fwd[l], "alog_b": a_log_bwd[l],
        "ssd_cw": jnp.pad(w_ssd_conv[l], ((0, 5), (0, 0))), "ssd_cb": row(b_ssd_conv[l]),
        "dskip": row(jnp.repeat(d_skip[l], SSD_HEAD_DIM)), "ssd_norm": row(ssd_norm[l]),
        "cf_w": cf_w, "cf_b": row(b_cf_conv[l]), "cf_g": row(cf_ln_g[l]),
        "cf_beta": row(cf_ln_b[l]),
        "wo": wo,
        "wup": w_ffn_up[l].astype(BF16), "ffn_cw": ffn_cw, "ffn_cb": row(b_ffn_conv[l]),
        "wd": w_ffn_down[l].astype(BF16),
    }


def kernel(x_prompt, x_sample, state_ssd_fwd, state_ssd_bwd, c, c_ctx, w_ada, b_ada, norm_mix_pre, norm_mix_post, w_in, w_ssd_conv, b_ssd_conv, a_log_fwd, a_log_bwd, dt_bias_fwd, dt_bias_bwd, d_skip, ssd_norm, w_cf_conv, b_cf_conv, cf_ln_g, cf_ln_b, w_out, norm_ffn_pre, norm_ffn_post, w_ffn_up, w_ffn_conv, b_ffn_conv, w_ffn_down):
    depth = w_ada.shape[0]
    dec_batch = x_sample.shape[0]
    rows = x_sample.shape[1] // GRID_W
    ctx_row = dec_batch
    n_cond = -(-(dec_batch + 1) // 8) * 8
    cond = jnp.zeros((n_cond, D_MODEL), F32)
    cond = cond.at[:dec_batch].set(c).at[ctx_row].set(c_ctx)

    xp, xl = x_prompt, x_sample
    new_f, new_b = [], []
    for l in range(depth):
        w = _layer_weights(l, w_in, w_ssd_conv, b_ssd_conv, a_log_fwd, a_log_bwd,
                           dt_bias_fwd, dt_bias_bwd, d_skip, ssd_norm, w_cf_conv, b_cf_conv,
                           cf_ln_g, cf_ln_b, w_out, norm_mix_pre, norm_mix_post,
                           norm_ffn_pre, norm_ffn_post, w_ffn_up, w_ffn_conv, b_ffn_conv,
                           w_ffn_down)
        mod = _modulation(cond, w_ada[l], b_ada[l])
        mod = [m.reshape(n_cond, 1, D_MODEL) for m in jnp.split(mod, 6, axis=-1)]
        xp, s_f, s_b = _trunk_layer(xp, mod, lambda b: ctx_row, w, None, None, None)
        new_f.append(_state_from_kernel_layout(s_f))
        new_b.append(_state_from_kernel_layout(s_b))
        xl, _, _ = _trunk_layer(xl, mod, lambda b: b, w,
                                _state_to_kernel_layout(state_ssd_fwd[:, l]),
                                _state_to_kernel_layout(state_ssd_bwd[:, l]), rows)
    return (xp, xl, jnp.stack(new_f, axis=1), jnp.stack(new_b, axis=1))
```

```python
import functools

import jax
import jax.numpy as jnp
from jax import lax
from jax.experimental import pallas as pl
from jax.experimental.pallas import tpu as pltpu

D_MODEL = 1024
GRID_W = 64
SSD_WIDTH = 1024
SSD_HEAD_DIM = 64
SSD_HEADS = 16
N_GROUPS = 2
D_STATE = 128
CHUNK = 128
CONV_CH = SSD_WIDTH + 2 * N_GROUPS * D_STATE
CF_WIDTH = 1024
CF_KERNEL = 31
D_FF = 2816
EPS = 1e-6

LANES = 128
SUBLANES = 8
HALO = 16
VMEM_LIMIT = 56 * 1024 * 1024

F32 = jnp.float32
BF16 = jnp.bfloat16
HIGHEST = lax.Precision.HIGHEST


def _params(n_axes):
    return pltpu.CompilerParams(
        dimension_semantics=("arbitrary",) * n_axes, vmem_limit_bytes=VMEM_LIMIT)


def _const_spec(shape):
    nd = len(shape)
    return pl.BlockSpec(shape, lambda *_: (0,) * nd, pipeline_mode=pl.Buffered(1))


def _silu(v):
    return v * jax.nn.sigmoid(v)


def _softplus(v):
    return jnp.maximum(v, 0.0) + jnp.log1p(jnp.exp(-jnp.abs(v)))


def _dot(a, b):
    return jnp.dot(a, b, preferred_element_type=F32)


def _mod_kernel(c_ref, w_ref, b_ref, o_ref):
    s = _silu(c_ref[...])
    o_ref[...] = jnp.dot(s, w_ref[...], precision=HIGHEST,
                         preferred_element_type=F32) + b_ref[...]


def _modulation(cond, w_ada, b_ada, layer):
    rows = cond.shape[0]
    n = w_ada.shape[2]
    tn = 1024
    return pl.pallas_call(
        _mod_kernel,
        grid=(n // tn,),
        in_specs=[_const_spec((rows, D_MODEL)),
                  pl.BlockSpec((None, D_MODEL, tn), lambda j: (layer, 0, j)),
                  pl.BlockSpec((1, tn), lambda j: (0, j))],
        out_specs=pl.BlockSpec((rows, tn), lambda j: (0, j)),
        out_shape=jax.ShapeDtypeStruct((rows, n), F32),
        compiler_params=_params(1),
        name="modulation",
    )(cond, w_ada, b_ada.reshape(1, n))


XBC_SPLIT = 3
IN_PROJ_TILE = 1024


def _in_proj_kernel(tm, tps, x_ref, xp_ref, xn_ref, sc_ref, sh_ref, g_ref,
                    wz_ref, wx_ref, wdt_ref, wa_ref, wg_ref, bdt_ref, cw_ref, cb_ref,
                    z_ref, xbc_ref, dtf_ref, dtb_ref, dtft_ref, dtbt_ref, u_ref,
                    hb_ref, *ext_refs):
    t = pl.program_id(0)
    first = (t % tps) == 0
    last = (t % tps) == tps - 1
    scale = g_ref[...] * (1.0 + sc_ref[...])
    shift = sh_ref[...]

    def mod_norm(x):
        ms = jnp.mean(x * x, axis=-1, keepdims=True)
        return (x * lax.rsqrt(ms + EPS) * scale + shift).astype(BF16)

    zero = jnp.zeros((HALO, D_MODEL), BF16)
    hb_ref[0:HALO, :] = jnp.where(first, zero, mod_norm(xp_ref[...]))
    hb_ref[HALO:HALO + tm, :] = mod_norm(x_ref[...])
    hb_ref[HALO + tm:2 * HALO + tm, :] = jnp.where(last, zero, mod_norm(xn_ref[...]))
    hb = hb_ref[HALO:HALO + tm, :]

    piece = CONV_CH // XBC_SPLIT

    def project(p):
        ext_refs[p][...] = _dot(hb_ref[...], wx_ref[:, p * piece:(p + 1) * piece])

    def conv(p):
        cols = slice(p * piece, (p + 1) * piece)
        cw = cw_ref[:, cols]
        ext_ref = ext_refs[p]
        xc = (cw[1:2, :] * ext_ref[HALO:HALO + tm, :] + cb_ref[:, cols]
              + cw[0:1, :] * ext_ref[HALO - 1:HALO - 1 + tm, :]
              + cw[2:3, :] * ext_ref[HALO + 1:HALO + 1 + tm, :])
        xbc_ref[:, cols] = _silu(xc).astype(BF16)

    project(0)
    for p in range(1, XBC_SPLIT):
        project(p)
        conv(p - 1)
    z_ref[...] = _dot(hb, wz_ref[...]).astype(BF16)
    conv(XBC_SPLIT - 1)

    a = _dot(hb, wa_ref[...])
    g = _dot(hb, wg_ref[...])
    u_ref[...] = (a * jax.nn.sigmoid(g)).astype(BF16)
    dt = _softplus(_dot(hb, wdt_ref[...]) + bdt_ref[...])
    dtf, dtb = dt[:, :LANES], dt[:, LANES:]
    dtf_ref[...] = dtf
    dtb_ref[...] = dtb
    dtft_ref[...] = dtf.T[:SSD_HEADS, :]
    dtbt_ref[...] = dtb.T[:SSD_HEADS, :]


def _in_proj(x2d, seqlen, mod_row, tm, sc, sh, g, w):
    n = x2d.shape[0]
    hb = tm // HALO
    n_halo = n // HALO
    mod_spec = pl.BlockSpec((None, 1, D_MODEL), lambda i: (mod_row(i), 0, 0))
    tok = lambda width: pl.BlockSpec((tm, width), lambda i: (i, 0))
    tok_t = pl.BlockSpec((SSD_HEADS, tm), lambda i: (0, i))
    prev_spec = pl.BlockSpec((HALO, D_MODEL), lambda i: (jnp.maximum(i * hb - 1, 0), 0))
    next_spec = pl.BlockSpec((HALO, D_MODEL),
                             lambda i: (jnp.minimum((i + 1) * hb, n_halo - 1), 0))
    weights = [w["wz"], w["wx"], w["wdt"], w["wa"], w["wg"], w["bdt"], w["ssd_cw"], w["ssd_cb"]]
    return pl.pallas_call(
        functools.partial(_in_proj_kernel, tm, seqlen // tm),
        grid=(n // tm,),
        in_specs=[tok(D_MODEL), prev_spec, next_spec, mod_spec, mod_spec,
                  _const_spec((1, D_MODEL))]
                 + [_const_spec(a.shape) for a in weights],
        scratch_shapes=[pltpu.VMEM((tm + 2 * HALO, D_MODEL), BF16)]
                       + [pltpu.VMEM((tm + 2 * HALO, CONV_CH // XBC_SPLIT), F32)] * XBC_SPLIT,
        out_specs=[tok(SSD_WIDTH), tok(CONV_CH), tok(LANES), tok(LANES), tok_t, tok_t,
                   tok(CF_WIDTH)],
        out_shape=[jax.ShapeDtypeStruct((n, SSD_WIDTH), BF16),
                   jax.ShapeDtypeStruct((n, CONV_CH), BF16),
                   jax.ShapeDtypeStruct((n, LANES), F32),
                   jax.ShapeDtypeStruct((n, LANES), F32),
                   jax.ShapeDtypeStruct((SSD_HEADS, n), F32),
                   jax.ShapeDtypeStruct((SSD_HEADS, n), F32),
                   jax.ShapeDtypeStruct((n, CF_WIDTH), BF16)],
        compiler_params=_params(1),
        name="in_proj",
    )(x2d, x2d, x2d, sc, sh, g, *weights)


LOG2E = 1.4426950408889634
N_PAIRS = SSD_HEADS // 2
SSD_CHUNKS_PER_STEP = 4


def _ssd_kernel(fwd, zero_init, cps, *refs):
    refs = list(refs)
    xbc_ref, dt_ref, dtt_ref, alr_ref, alc_ref = refs[:5]
    del refs[:5]
    init_ref = None if zero_init else refs.pop(0)
    if fwd:
        yb_ref, z_ref, dskip_ref, nrm_ref = refs[:4]
        del refs[:4]
    y_ref, fin_ref = refs[:2]
    s_refs = refs[2:2 + N_PAIRS]
    yacc_refs = refs[2 + N_PAIRS:]
    j = pl.program_id(1)

    @pl.when(j == 0)
    def _():
        for pair in range(N_PAIRS):
            if zero_init:
                s_refs[pair][...] = jnp.zeros((D_STATE, LANES), F32)
            else:
                s_refs[pair][...] = init_ref[pair * LANES:(pair + 1) * LANES, :].T

    ri = lax.broadcasted_iota(jnp.int32, (CHUNK, CHUNK), 0)
    ci = lax.broadcasted_iota(jnp.int32, (CHUNK, CHUNK), 1)
    keep = (ri >= ci) if fwd else (ri <= ci)
    tri = keep.astype(F32)
    lane = lax.broadcasted_iota(jnp.int32, (1, LANES), 1)
    a_row = jnp.where(lane < SSD_HEADS, -jnp.exp(alr_ref[...]) * LOG2E, 0.0)
    a_col = -jnp.exp(alc_ref[...]) * LOG2E
    half = lane < SSD_HEAD_DIM
    nt = (((1,), (1,)), ((), ()))
    heads_per_group = SSD_HEADS // N_GROUPS
    end = CHUNK - 1 if fwd else 0

    pre = []
    for sc in range(cps):
        rows = slice(sc * CHUNK, (sc + 1) * CHUNK)
        dt_row = dtt_ref[:, rows]
        cum_col = jnp.dot(tri, dt_ref[rows, :] * a_row, precision=HIGHEST,
                          preferred_element_type=F32)
        cum_row = lax.dot_general(dt_row * a_col, tri, nt, precision=HIGHEST,
                                  preferred_element_type=F32)
        cum_end = cum_row[:, end:end + 1]
        p = {
            "cum_col": cum_col,
            "wgt_row": jnp.exp2(cum_end - cum_row) * dt_row,
            "edec": jnp.exp2(cum_end),
            "ecol": jnp.exp2(cum_col),
            "src_row": cum_row - jnp.log2(dt_row),
            "cb": [], "cg": [], "bgt": [],
        }
        for grp in range(N_GROUPS):
            b_cols = slice(SSD_WIDTH + grp * D_STATE, SSD_WIDTH + (grp + 1) * D_STATE)
            c_cols = slice(SSD_WIDTH + (N_GROUPS + grp) * D_STATE,
                           SSD_WIDTH + (N_GROUPS + grp + 1) * D_STATE)
            p["cb"].append(lax.dot_general(xbc_ref[rows, c_cols], xbc_ref[rows, b_cols], nt,
                                           preferred_element_type=F32))
            p["cg"].append(xbc_ref[rows, c_cols].astype(F32))
            p["bgt"].append(xbc_ref[rows, b_cols].astype(F32).T)
        pre.append(p)

    for sc in (range(cps) if fwd else reversed(range(cps))):
        rows = slice(sc * CHUNK, (sc + 1) * CHUNK)
        p = pre[sc]
        cum_col, wgt_row, edec, ecol, src_row = (
            p["cum_col"], p["wgt_row"], p["edec"], p["ecol"], p["src_row"])
        for grp in range(N_GROUPS):
            cb, cg, bgt = p["cb"][grp], p["cg"][grp], p["bgt"][grp]
            for pair in range(grp * heads_per_group // 2, (grp + 1) * heads_per_group // 2):
                lanes = slice(pair * LANES, (pair + 1) * LANES)
                xp = xbc_ref[rows, lanes]
                zero = jnp.zeros_like(xp)
                x_bd = jnp.concatenate(
                    [jnp.where(half, xp, zero), jnp.where(half, zero, xp)], axis=0)
                sp = s_refs[pair][...]
                spb = sp.astype(BF16)
                s_bd = jnp.concatenate(
                    [jnp.where(half, spb, zero), jnp.where(half, zero, spb)], axis=0)
                m_parts, c_parts, b_parts = [], [], []
                for h in (2 * pair, 2 * pair + 1):
                    seg = jnp.where(keep, cum_col[:, h:h + 1] - src_row[h:h + 1, :], -1e30)
                    m_parts.append((cb * jnp.exp2(seg)).astype(BF16))
                    c_parts.append((cg * ecol[:, h:h + 1]).astype(BF16))
                    b_parts.append((bgt * wgt_row[h:h + 1, :]).astype(BF16))
                lhs = jnp.concatenate(m_parts + c_parts, axis=1)
                rhs = jnp.concatenate([x_bd, s_bd], axis=0)
                y_pair = _dot(lhs, rhs)
                if fwd:
                    yacc_refs[sc * N_PAIRS + pair][...] = (
                        y_pair + xp.astype(F32) * dskip_ref[:, lanes])
                else:
                    y_ref[rows, lanes] = y_pair.astype(BF16)
                dec = jnp.where(half, edec[2 * pair:2 * pair + 1, :],
                                edec[2 * pair + 1:2 * pair + 2, :])
                s_refs[pair][...] = sp * dec + _dot(jnp.concatenate(b_parts, axis=1), x_bd)

        if fwd:
            yz = []
            for pair in range(N_PAIRS):
                lanes = slice(pair * LANES, (pair + 1) * LANES)
                y = yacc_refs[sc * N_PAIRS + pair][...] + yb_ref[rows, lanes].astype(F32)
                yz.append(y * _silu(z_ref[rows, lanes].astype(F32)))
            sq = yz[0] * yz[0]
            for v in yz[1:]:
                sq = sq + v * v
            rstd = lax.rsqrt(jnp.sum(sq, axis=-1, keepdims=True) * (1.0 / SSD_WIDTH) + EPS)
            for pair in range(N_PAIRS):
                lanes = slice(pair * LANES, (pair + 1) * LANES)
                y_ref[rows, lanes] = (yz[pair] * rstd * nrm_ref[:, lanes]).astype(BF16)

    @pl.when(j == pl.num_programs(1) - 1)
    def _():
        for pair in range(N_PAIRS):
            fin_ref[pair * LANES:(pair + 1) * LANES, :] = s_refs[pair][...].T


def _ssd_sweep(fwd, zero_init, bsz, seqlen, xbc, dt, dtt, alog, init, extra):
    cps = min(SSD_CHUNKS_PER_STEP, seqlen // CHUNK)
    blk = cps * CHUNK
    ns = seqlen // blk

    def block_of(j):
        return j if fwd else ns - 1 - j

    tok = lambda width: pl.BlockSpec(
        (blk, width), lambda b, j: (b * ns + block_of(j), 0))
    dtt_spec = pl.BlockSpec((SSD_HEADS, blk), lambda b, j: (0, b * ns + block_of(j)))
    state_spec = pl.BlockSpec((None, SSD_WIDTH, D_STATE), lambda b, j: (b, 0, 0))
    alog_row = jnp.pad(alog.reshape(1, SSD_HEADS), ((0, 0), (0, LANES - SSD_HEADS)))
    alog_col = alog.reshape(SSD_HEADS, 1)
    in_specs = [tok(CONV_CH), tok(LANES), dtt_spec,
                _const_spec((1, LANES)), _const_spec((SSD_HEADS, 1))]
    args = [xbc, dt, dtt, alog_row, alog_col]
    if not zero_init:
        in_specs.append(state_spec)
        args.append(init)
    if fwd:
        yb, z, dskip, nrm = extra
        in_specs += [tok(SSD_WIDTH), tok(SSD_WIDTH), _const_spec((1, SSD_WIDTH)),
                     _const_spec((1, SSD_WIDTH))]
        args += [yb, z, dskip, nrm]
    return pl.pallas_call(
        functools.partial(_ssd_kernel, fwd, zero_init, cps),
        grid=(bsz, ns),
        in_specs=in_specs,
        out_specs=[tok(SSD_WIDTH), state_spec],
        out_shape=[jax.ShapeDtypeStruct((bsz * seqlen, SSD_WIDTH), BF16),
                   jax.ShapeDtypeStruct((bsz, SSD_WIDTH, D_STATE), F32)],
        scratch_shapes=[pltpu.VMEM((D_STATE, LANES), F32)] * N_PAIRS
                       + [pltpu.VMEM((CHUNK, LANES), F32)] * (cps * N_PAIRS if fwd else 0),
        compiler_params=_params(2),
        name="ssd_fwd" if fwd else "ssd_bwd",
    )(*args)


def _cf_pitches(tm):
    seg = tm // SUBLANES
    in_pitch = -(-(seg + CF_KERNEL - 1 - 4) // 8) * 8 + 4
    out_pitch = seg + 8
    return seg, in_pitch, out_pitch


def _cf_kernel(tm, tps, u_ref, up_ref, un_ref, w_ref, b_ref, g_ref, beta_ref, o_ref,
               buf_ref, slab_ref, acc_ref):
    seg, in_pitch, out_pitch = _cf_pitches(tm)
    pad = (CF_KERNEL - 1) // 2
    t = pl.program_id(0)
    first = (t % tps) == 0
    last = (t % tps) == tps - 1
    buf_ref[0:HALO, :] = jnp.where(first, 0.0, up_ref[...].astype(F32))
    buf_ref[HALO:HALO + tm, :] = u_ref[...].astype(F32)
    buf_ref[HALO + tm:2 * HALO + tm, :] = jnp.where(last, 0.0, un_ref[...].astype(F32))
    span = seg + 2 * pad
    for jb in range(CF_WIDTH // LANES):
        for r in range(SUBLANES):
            src = HALO - pad + r * seg
            slab_ref[jb, r * in_pitch:r * in_pitch + span, :] = (
                buf_ref[src:src + span, jb * LANES:(jb + 1) * LANES])

    def lane_block(jb, carry):
        l0 = pl.multiple_of(jb * LANES, LANES)
        w = w_ref[:, pl.ds(l0, LANES)]
        taps = [jnp.broadcast_to(w[k:k + 1, :], (SUBLANES, LANES)) for k in range(CF_KERNEL)]
        bias = jnp.broadcast_to(b_ref[:, pl.ds(l0, LANES)], (SUBLANES, LANES))
        for i in range(seg):
            acc = bias
            for k in range(CF_KERNEL):
                acc = acc + taps[k] * slab_ref[jb, pl.ds(i + k, SUBLANES, stride=in_pitch), :]
            acc_ref[jb, pl.ds(i, SUBLANES, stride=out_pitch), :] = acc
        return carry

    lax.fori_loop(0, CF_WIDTH // LANES, lane_block, 0)

    nb = CF_WIDTH // LANES
    for r in range(SUBLANES):
        v = [acc_ref[jb, r * out_pitch:r * out_pitch + seg, :] for jb in range(nb)]
        mu = jnp.sum(sum(v[1:], v[0]), axis=-1, keepdims=True) * (1.0 / CF_WIDTH)
        d = [vj - mu for vj in v]
        sq = d[0] * d[0]
        for dj in d[1:]:
            sq = sq + dj * dj
        rstd = lax.rsqrt(jnp.sum(sq, axis=-1, keepdims=True) * (1.0 / CF_WIDTH) + EPS)
        for jb in range(nb):
            lanes = slice(jb * LANES, (jb + 1) * LANES)
            y = d[jb] * rstd * g_ref[:, lanes] + beta_ref[:, lanes]
            o_ref[r * seg:(r + 1) * seg, lanes] = _silu(y).astype(BF16)


def _cf_module(u, seqlen, tm, w, b, g, beta):
    n = u.shape[0]
    tps = seqlen // tm
    hb = tm // HALO
    n_halo = n // HALO
    _, in_pitch, out_pitch = _cf_pitches(tm)
    nb = CF_WIDTH // LANES
    return pl.pallas_call(
        functools.partial(_cf_kernel, tm, tps),
        grid=(n // tm,),
        in_specs=[pl.BlockSpec((tm, CF_WIDTH), lambda t: (t, 0)),
                  pl.BlockSpec((HALO, CF_WIDTH), lambda t: (jnp.maximum(t * hb - 1, 0), 0)),
                  pl.BlockSpec((HALO, CF_WIDTH),
                               lambda t: (jnp.minimum((t + 1) * hb, n_halo - 1), 0)),
                  _const_spec(w.shape), _const_spec(b.shape), _const_spec(g.shape),
                  _const_spec(beta.shape)],
        out_specs=pl.BlockSpec((tm, CF_WIDTH), lambda t: (t, 0)),
        out_shape=jax.ShapeDtypeStruct((n, CF_WIDTH), BF16),
        scratch_shapes=[pltpu.VMEM((tm + 2 * HALO, CF_WIDTH), F32),
                        pltpu.VMEM((nb, SUBLANES * in_pitch, LANES), F32),
                        pltpu.VMEM((nb, SUBLANES * out_pitch, LANES), F32)],
        compiler_params=_params(1),
        name="cf_module",
    )(u, u, u, w, b, g, beta)


MLP_KW = 256
MLP_TILE = 512
CF_TILE = 1024
GROUP = SUBLANES * SUBLANES


def _mlp_kernel(grid2d, nseg, tpi, *refs):
    refs = list(refs)
    y_ref, u_ref, x_ref = refs[:3]
    del refs[:3]
    if grid2d:
        yp_ref, up_ref, xp_ref, yn_ref, un_ref, xn_ref = refs[:6]
        del refs[:6]
    (g1_ref, sc_ref, sh_ref, g2_ref, wo_ref, gpost1_ref, gpre2_ref, wup_ref,
     cw_ref, cb_ref, wd_ref, gpost2_ref, o_ref) = refs[:13]
    yu_ref, xext_ref, hs_ref, hb_ref, x1_ref, blk_a, blk_b, act_ref, fs_ref = refs[13:]
    tm = GROUP * nseg
    lo = GROUP if grid2d else 0
    ext = tm + 2 * lo
    nlb = D_MODEL // LANES
    t = pl.program_id(0)
    top = (t % tpi) == 0
    bottom = (t % tpi) == tpi - 1

    yu_ref[lo:lo + tm, 0:SSD_WIDTH] = y_ref[...]
    yu_ref[lo:lo + tm, SSD_WIDTH:] = u_ref[...]
    xext_ref[lo:lo + tm, :] = x_ref[...]
    if grid2d:
        yu_ref[0:lo, 0:SSD_WIDTH] = yp_ref[...]
        yu_ref[0:lo, SSD_WIDTH:] = up_ref[...]
        xext_ref[0:lo, :] = xp_ref[...]
        yu_ref[lo + tm:, 0:SSD_WIDTH] = yn_ref[...]
        yu_ref[lo + tm:, SSD_WIDTH:] = un_ref[...]
        xext_ref[lo + tm:, :] = xn_ref[...]
    mix = _dot(yu_ref[...], wo_ref[...])
    ms = jnp.mean(mix * mix, axis=-1, keepdims=True)
    x1 = xext_ref[...] + g1_ref[...] * (mix * lax.rsqrt(ms + EPS) * gpost1_ref[...])
    x1_ref[...] = x1[lo:lo + tm, :]
    ms2 = jnp.mean(x1 * x1, axis=-1, keepdims=True)
    h = x1 * lax.rsqrt(ms2 + EPS) * (gpre2_ref[...] * (1.0 + sc_ref[...])) + sh_ref[...]

    for j in range(nlb):
        hs_ref[j] = h[:, j * LANES:(j + 1) * LANES]
    for j in range(nlb):
        for g in range(ext // GROUP):
            grp = jnp.concatenate(
                [hs_ref[j, pl.ds(g * GROUP + i, SUBLANES, stride=SUBLANES), :]
                 for i in range(SUBLANES)], axis=0)
            if grid2d and g == 0:
                grp = jnp.where(top, 0.0, grp)
            if grid2d and g == ext // GROUP - 1:
                grp = jnp.where(bottom, 0.0, grp)
            hb_ref[g * GROUP:(g + 1) * GROUP, j * LANES:(j + 1) * LANES] = grp.astype(BF16)

    sub = lax.broadcasted_iota(jnp.int32, (SUBLANES, 1), 0)

    def shift_down(v, fill):
        edge = 0.0 if fill is None else pltpu.roll(fill, 1, axis=0)
        return jnp.where(sub > 0, pltpu.roll(v, 1, axis=0), edge)

    def shift_up(v, fill):
        edge = 0.0 if fill is None else pltpu.roll(fill, SUBLANES - 1, axis=0)
        return jnp.where(sub < SUBLANES - 1, pltpu.roll(v, SUBLANES - 1, axis=0), edge)

    def project(k0, blk):
        blk[:, 0:MLP_KW] = _dot(hb_ref[...], wup_ref[:, pl.ds(k0, MLP_KW)])
        blk[:, MLP_KW:2 * MLP_KW] = _dot(hb_ref[...], wup_ref[:, pl.ds(D_FF + k0, MLP_KW)])

    def conv_vreg(blk, s, i, cols, w, bias, loaded):
        def column(g, k, dc):
            def vreg(gg):
                if (gg, k) not in loaded:
                    r0 = gg * GROUP + k * SUBLANES
                    loaded[(gg, k)] = blk[r0:r0 + SUBLANES, cols]
                return loaded[(gg, k)]
            if grid2d:
                acc = w[dc:dc + 1, :] * vreg(g)
                for dr in (1, 2):
                    acc = acc + w[3 * dr + dc:3 * dr + dc + 1, :] * vreg(g + dr)
                return acc
            return w[3 + dc:4 + dc, :] * vreg(g)

        if i > 0:
            left = column(s, i - 1, 0)
        else:
            before = column(s - 1, SUBLANES - 1, 0) if (not grid2d and s % tpi > 0) else None
            left = shift_down(column(s, SUBLANES - 1, 0), before)
        if i < SUBLANES - 1:
            right = column(s, i + 1, 2)
        else:
            after = column(s + 1, 0, 2) if (not grid2d and (s + 1) % tpi > 0) else None
            right = shift_up(column(s, 0, 2), after)
        return column(s, i, 1) + bias + left + right

    def gate_slice(k0, blk):
        for q in range(MLP_KW // LANES):
            lg = pl.multiple_of(k0 + q * LANES, LANES)
            lv = pl.multiple_of(D_FF + k0 + q * LANES, LANES)
            wg, wv = cw_ref[:, pl.ds(lg, LANES)], cw_ref[:, pl.ds(lv, LANES)]
            bg, bv = cb_ref[:, pl.ds(lg, LANES)], cb_ref[:, pl.ds(lv, LANES)]
            cg = slice(q * LANES, (q + 1) * LANES)
            cv = slice(MLP_KW + q * LANES, MLP_KW + (q + 1) * LANES)
            gate_in, val_in = {}, {}
            for s in range(nseg):
                for i0 in range(0, SUBLANES, 2):
                    act = jnp.concatenate(
                        [_silu(conv_vreg(blk, s, i, cg, wg, bg, gate_in))
                         * conv_vreg(blk, s, i, cv, wv, bv, val_in) for i in (i0, i0 + 1)],
                        axis=0)
                    r0 = s * GROUP + i0 * SUBLANES
                    act_ref[r0:r0 + 2 * SUBLANES, pl.ds(lg, LANES)] = act.astype(BF16)

    def slice_pair(jj, carry):
        k0 = pl.multiple_of(jj * (2 * MLP_KW), 2 * MLP_KW)
        project(k0 + MLP_KW, blk_b)
        gate_slice(k0, blk_a)
        project(k0 + 2 * MLP_KW, blk_a)
        gate_slice(k0 + MLP_KW, blk_b)
        return carry

    n_slices = D_FF // MLP_KW
    assert n_slices % 2 == 1 and n_slices * MLP_KW == D_FF
    project(0, blk_a)
    lax.fori_loop(0, (n_slices - 1) // 2, slice_pair, 0)
    gate_slice((n_slices - 1) * MLP_KW, blk_a)

    f = _dot(act_ref[...], wd_ref[...])
    msf = jnp.mean(f * f, axis=-1, keepdims=True)
    fn = f * lax.rsqrt(msf + EPS) * gpost2_ref[...]
    for j in range(nlb):
        for g in range(nseg):
            for i in range(SUBLANES):
                r0 = g * GROUP + i * SUBLANES
                fs_ref[j, pl.ds(g * GROUP + i, SUBLANES, stride=SUBLANES), :] = (
                    fn[r0:r0 + SUBLANES, j * LANES:(j + 1) * LANES])
    for j in range(nlb):
        lanes = slice(j * LANES, (j + 1) * LANES)
        o_ref[:, lanes] = x1_ref[:, lanes] + g2_ref[:, lanes] * fs_ref[j]


def _mlp(y, u, x2d, mod_row, grid2d, nseg, tpi, g1, sc, sh, g2, w):
    n = x2d.shape[0]
    seg = GROUP
    tm = seg * nseg
    ext = tm + 2 * seg if grid2d else tm
    nlb = D_MODEL // LANES
    n_seg_total = n // seg
    mod_spec = pl.BlockSpec((None, 1, D_MODEL), lambda i: (mod_row(i), 0, 0))
    tok = pl.BlockSpec((tm, D_MODEL), lambda i: (i, 0))
    prev_spec = pl.BlockSpec((seg, D_MODEL), lambda i: (jnp.maximum(i * nseg - 1, 0), 0))
    next_spec = pl.BlockSpec(
        (seg, D_MODEL), lambda i: (jnp.minimum((i + 1) * nseg, n_seg_total - 1), 0))
    halo_specs = [prev_spec] * 3 + [next_spec] * 3 if grid2d else []
    halo_args = [y, u, x2d, y, u, x2d] if grid2d else []
    weights = [w["wo"], w["gpost1"], w["gpre2"], w["wup"], w["ffn_cw"],
               w["ffn_cb"], w["wd"], w["gpost2"]]
    return pl.pallas_call(
        functools.partial(_mlp_kernel, grid2d, nseg, tpi),
        grid=(n // tm,),
        in_specs=[tok, tok, tok] + halo_specs + [mod_spec] * 4
                 + [_const_spec(a.shape) for a in weights],
        out_specs=tok,
        out_shape=jax.ShapeDtypeStruct((n, D_MODEL), F32),
        scratch_shapes=[pltpu.VMEM((ext, SSD_WIDTH + CF_WIDTH), BF16),
                        pltpu.VMEM((ext, D_MODEL), F32),
                        pltpu.VMEM((nlb, ext, LANES), F32),
                        pltpu.VMEM((ext, D_MODEL), BF16),
                        pltpu.VMEM((tm, D_MODEL), F32),
                        pltpu.VMEM((ext, 2 * MLP_KW), F32),
                        pltpu.VMEM((ext, 2 * MLP_KW), F32),
                        pltpu.VMEM((tm, D_FF), BF16),
                        pltpu.VMEM((nlb, tm, LANES), F32)],
        compiler_params=_params(1),
        name="mlp_grid" if grid2d else "mlp_seq",
    )(y, u, x2d, *halo_args, g1, sc, sh, g2, *weights)


def _state_to_kernel_layout(s):
    return s.reshape(s.shape[0], SSD_WIDTH, D_STATE)


def _state_from_kernel_layout(s):
    return s.reshape(s.shape[0], SSD_HEADS, SSD_HEAD_DIM, D_STATE)


def _trunk_layer(x, mod, mod_row_of_batch, w, init_f, init_b, rows):
    bsz, seqlen, _ = x.shape
    x2d = x.reshape(bsz * seqlen, D_MODEL)
    sh1, sc1, g1, sh2, sc2, g2 = mod
    latent = rows is not None
    tm = 512 if latent else seqlen
    tps = seqlen // tm
    mod_row = lambda i: mod_row_of_batch(i // tps)

    tm_in = min(IN_PROJ_TILE, seqlen)
    z, xbc, dtf, dtb, dtft, dtbt, u = _in_proj(
        x2d, seqlen, lambda i: mod_row_of_batch(i // (seqlen // tm_in)), tm_in, sc1, sh1,
        w["gpre1"], w)

    zero_init = init_f is None
    yb, fin_b = _ssd_sweep(False, zero_init, bsz, seqlen, xbc, dtb, dtbt, w["alog_b"],
                           init_b, None)
    y, fin_f = _ssd_sweep(True, zero_init, bsz, seqlen, xbc, dtf, dtft, w["alog_f"],
                          init_f, (yb, z, w["dskip"], w["ssd_norm"]))

    ucf = _cf_module(u, seqlen, min(CF_TILE, seqlen), w["cf_w"], w["cf_b"], w["cf_g"],
                     w["cf_beta"])
    nseg = MLP_TILE // GROUP
    if latent:
        assert GRID_W == GROUP and tm == MLP_TILE
        out = _mlp(y, ucf, x2d, mod_row, True, nseg, rows // nseg, g1, sc2, sh2, g2, w)
    else:
        assert MLP_TILE % seqlen == 0
        out = _mlp(y, ucf, x2d, mod_row, False, nseg, seqlen // GROUP, g1, sc2, sh2, g2, w)
    return out.reshape(bsz, seqlen, D_MODEL), fin_f, fin_b


def _layer_weights(l, w_in, w_ssd_conv, b_ssd_conv, a_log_fwd, a_log_bwd, dt_bias_fwd,
                   dt_bias_bwd, d_skip, ssd_norm, w_cf_conv, b_cf_conv, cf_ln_g, cf_ln_b,
                   w_out, norm_mix_pre, norm_mix_post, norm_ffn_pre, norm_ffn_post,
                   w_ffn_up, w_ffn_conv, b_ffn_conv, w_ffn_down):
    wi = w_in[l].astype(BF16)
    o = 0
    wz = wi[:, o:o + SSD_WIDTH]; o += SSD_WIDTH
    wx = wi[:, o:o + CONV_CH]; o += CONV_CH
    wdf = wi[:, o:o + SSD_HEADS]; o += SSD_HEADS
    wdb = wi[:, o:o + SSD_HEADS]; o += SSD_HEADS
    wa = wi[:, o:o + CF_WIDTH]; o += CF_WIDTH
    wg = wi[:, o:o + CF_WIDTH]
    pad_l = lambda a: jnp.pad(a, ((0, 0), (0, LANES - a.shape[1])))
    row = lambda a: a.reshape(1, -1)
    wo = w_out[l].astype(BF16)
    cf_w = jnp.pad(w_cf_conv[l], ((0, 32 - CF_KERNEL), (0, 0)))
    ffn_cw = jnp.pad(w_ffn_conv[l].reshape(9, 2 * D_FF), ((0, 7), (0, 0)))
    return {
        "wz": wz, "wx": wx, "wdt": jnp.concatenate([pad_l(wdf), pad_l(wdb)], axis=1),
        "wa": wa, "wg": wg,
        "bdt": jnp.concatenate([pad_l(row(dt_bias_fwd[l])), pad_l(row(dt_bias_bwd[l]))],
                               axis=1),
        "gpre1": row(norm_mix_pre[l]), "gpost1": row(norm_mix_post[l]),
        "gpre2": row(norm_ffn_pre[l]), "gpost2": row(norm_ffn_post[l]),
        "alog_f": a_log_fwd[l], "alog_b": a_log_bwd[l],
        "ssd_cw": jnp.pad(w_ssd_conv[l], ((0, 5), (0, 0))), "ssd_cb": row(b_ssd_conv[l]),
        "dskip": row(jnp.repeat(d_skip[l], SSD_HEAD_DIM)), "ssd_norm": row(ssd_norm[l]),
        "cf_w": cf_w, "cf_b": row(b_cf_conv[l]), "cf_g": row(cf_ln_g[l]),
        "cf_beta": row(cf_ln_b[l]),
        "wo": wo,
        "wup": w_ffn_up[l].astype(BF16), "ffn_cw": ffn_cw, "ffn_cb": row(b_ffn_conv[l]),
        "wd": w_ffn_down[l].astype(BF16),
    }


def kernel(x_prompt, x_sample, state_ssd_fwd, state_ssd_bwd, c, c_ctx, w_ada, b_ada, norm_mix_pre, norm_mix_post, w_in, w_ssd_conv, b_ssd_conv, a_log_fwd, a_log_bwd, dt_bias_fwd, dt_bias_bwd, d_skip, ssd_norm, w_cf_conv, b_cf_conv, cf_ln_g, cf_ln_b, w_out, norm_ffn_pre, norm_ffn_post, w_ffn_up, w_ffn_conv, b_ffn_conv, w_ffn_down):
    depth = w_ada.shape[0]
    dec_batch = x_sample.shape[0]
    rows = x_sample.shape[1] // GRID_W
    ctx_row = dec_batch
    n_cond = -(-(dec_batch + 1) // 8) * 8
    cond = jnp.zeros((n_cond, D_MODEL), F32)
    cond = cond.at[:dec_batch].set(c).at[ctx_row].set(c_ctx)

    xp, xl = x_prompt, x_sample
    new_f, new_b = [], []
    for l in range(depth):
        w = _layer_weights(l, w_in, w_ssd_conv, b_ssd_conv, a_log_fwd, a_log_bwd,
                           dt_bias_fwd, dt_bias_bwd, d_skip, ssd_norm, w_cf_conv, b_cf_conv,
                           cf_ln_g, cf_ln_b, w_out, norm_mix_pre, norm_mix_post,
                           norm_ffn_pre, norm_ffn_post, w_ffn_up, w_ffn_conv, b_ffn_conv,
                           w_ffn_down)
        mod = _modulation(cond, w_ada, b_ada[l], l)
        mod = [m.reshape(n_cond, 1, D_MODEL) for m in jnp.split(mod, 6, axis=-1)]
        xp, s_f, s_b = _trunk_layer(xp, mod, lambda b: ctx_row, w, None, None, None)
        new_f.append(_state_from_kernel_layout(s_f))
        new_b.append(_state_from_kernel_layout(s_b))
        xl, _, _ = _trunk_layer(xl, mod, lambda b: b, w,
                                _state_to_kernel_layout(state_ssd_fwd[:, l]),
                                _state_to_kernel_layout(state_ssd_bwd[:, l]), rows)
    return (xp, xl, jnp.stack(new_f, axis=1), jnp.stack(new_b, axis=1))
```
